```python
import math
import jax
import jax.numpy as jnp
from jax import lax
import numpy as np

D_MODEL = 1024
BATCH = 2
SEQ = 8192
DEPTH = 1
DEC_BATCH = 128
DEC_SEQ = 4
PAST_LEN = 8192
PAGE_SIZE = 128

D_MIX = D_MODEL
M_HEADS = 4
M_HEAD_DIM = D_MIX // 2 // M_HEADS
D_M = M_HEADS * M_HEAD_DIM
M_CHUNK = 64
CONV_W = 4
N_HEADS = 8
N_HEAD_DIM = (D_MIX - D_M) // N_HEADS
D_N = N_HEADS * N_HEAD_DIM
KV_HEADS = 2
GQA = N_HEADS // KV_HEADS
D_CMP = 16
L_CMP = 2 * D_CMP
L_SEL = 64
TOP_N = 16
WINDOW = 512
Q_BLOCK = 128
SEL_RATIO = L_SEL // D_CMP
SLC_W = tuple([1.0] + [2.0] * (SEL_RATIO - 1) + [1.0])
ROT_DIM = N_HEAD_DIM // 4
ROPE_THETA = 500000.0
NORM_EPS = 1e-6
NEG = -1e30
FORCE = 1e4
KV_W = 2 * KV_HEADS * N_HEAD_DIM
SPLITS = (D_M, D_M, D_M, D_M, D_M, M_HEADS, M_HEADS, D_N, KV_W, KV_W, KV_W, 3 * N_HEADS, D_N)
D_IN = sum(SPLITS)

kernel_name = 'hymba_mlstm_nsa_decode_step'


def rmsnorm(x, g):
    xf = x.astype(jnp.float32)
    xf = xf * lax.rsqrt(jnp.mean(xf * xf, axis=-1, keepdims=True) + NORM_EPS)
    return xf.astype(x.dtype) * g


def masked_softmax(s, mask):
    s = jnp.where(mask, s.astype(jnp.float32), NEG)
    return jnp.where(mask, jax.nn.softmax(s, axis=-1), 0.0)


def rope(x, pos):
    half = ROT_DIM // 2
    inv = ROPE_THETA ** (-jnp.arange(half, dtype=jnp.float32) / half)
    ang = pos.astype(jnp.float32)[:, None] * inv[None, :]
    cos = jnp.cos(ang)[None, :, None, :].astype(x.dtype)
    sin = jnp.sin(ang)[None, :, None, :].astype(x.dtype)
    x1 = x[..., :half]
    x2 = x[..., half:ROT_DIM]
    return jnp.concatenate([x1 * cos - x2 * sin, x2 * cos + x1 * sin, x[..., ROT_DIM:]], axis=-1)


def rope_kv(kv, pos):
    return jnp.stack([rope(kv[:, :, 0], pos), kv[:, :, 1]], axis=2)


def causal_conv(x, buf, w, b):
    T = x.shape[1]
    xp = jnp.concatenate([buf.astype(x.dtype), x], axis=1)
    y = b
    for j in range(CONV_W):
        y = y + xp[:, j:j + T] * w[j]
    return y, xp[:, -(CONV_W - 1):]


def mlstm_chunked(q, k, v, ig, fl, C0, n0, m0):
    B, T, NH, DH = q.shape
    L = math.gcd(T, M_CHUNK)
    nc = T // L

    def to_chunks(a):
        a = a.astype(jnp.float32)
        return jnp.moveaxis(a.reshape((B, nc, L) + a.shape[2:]), 1, 0)

    xs = (to_chunks(q), to_chunks(k), to_chunks(v), to_chunks(ig), to_chunks(fl))
    tri = jnp.tril(jnp.ones((L, L), dtype=bool))

    def step(carry, xc):
        C, n, m = carry
        qc, kc, vc, ic, fc = xc
        b = jnp.cumsum(fc, axis=1)
        log_d = b[:, :, None, :] - b[:, None, :, :] + ic[:, None, :, :]
        log_d = jnp.where(tri[None, :, :, None], log_d, NEG)
        m_inter = b + m[:, None, :]
        m_new = jnp.maximum(m_inter, jnp.max(log_d, axis=2))
        dmat = jnp.exp(log_d - m_new[:, :, None, :])
        w = jnp.einsum('bthd,bshd->btsh', qc, kc) * dmat
        inter = jnp.exp(m_inter - m_new)
        num = jnp.einsum('btsh,bshd->bthd', w, vc) + inter[..., None] * jnp.einsum('bhvk,bthk->bthv', C, qc)
        nq = jnp.sum(w, axis=2) + inter * jnp.einsum('bhk,bthk->bth', n, qc)
        h = num / jnp.maximum(jnp.abs(nq), jnp.exp(-m_new))[..., None]
        m_end = m_new[:, -1]
        b_end = b[:, -1]
        decay = jnp.exp(b_end + m - m_end)
        wk = jnp.exp(b_end[:, None, :] - b + ic - m_end[:, None, :])
        C = decay[..., None, None] * C + jnp.einsum('bsh,bshv,bshk->bhvk', wk, vc, kc)
        n = decay[..., None] * n + jnp.einsum('bsh,bshk->bhk', wk, kc)
        return (C, n, m_end), h

    (C, n, m), hs = lax.scan(step, (C0.astype(jnp.float32), n0.astype(jnp.float32), m0.astype(jnp.float32)), xs)
    h = jnp.moveaxis(hs, 0, 1).reshape(B, T, NH, DH)
    return h, C, n, m


def mlstm_group(q_m, k_m, v_m, o_m, i_m, f_m, conv_buf, C0, n0, m0, w_conv, b_conv, b_igate, b_fgate):
    B, T, _ = q_m.shape
    qk, new_buf = causal_conv(jnp.concatenate([q_m, k_m], axis=-1), conv_buf, w_conv, b_conv)
    qk = jax.nn.silu(qk)
    hs = (B, T, M_HEADS, M_HEAD_DIM)
    q = qk[..., :D_M].reshape(hs)
    k = qk[..., D_M:].reshape(hs) * (M_HEAD_DIM ** -0.5)
    v = v_m.reshape(hs)
    ig = (i_m + b_igate).astype(jnp.float32)
    fl = jax.nn.log_sigmoid((f_m + b_fgate).astype(jnp.float32))
    h, C, n, m = mlstm_chunked(q, k, v, ig, fl, C0, n0, m0)
    h = h.reshape(B, T, D_M).astype(q_m.dtype) * jax.nn.sigmoid(o_m)
    return h, new_buf, C, n, m


def cmp_sub(kv, w_cmp):
    B, T = kv.shape[:2]
    kb = kv.reshape(B, T // D_CMP, D_CMP, 2, KV_HEADS, N_HEAD_DIM)
    first = jnp.einsum('bnpchd,cpde->bnche', kb, w_cmp[:, :D_CMP])
    second = jnp.einsum('bnpchd,cpde->bnche', kb, w_cmp[:, D_CMP:])
    return first, second


def compress_blocks(first, second, w_cmp, pe_cmp):
    bias = jnp.einsum('cpd,cpde->ce', pe_cmp, w_cmp)
    return first[:, :-1] + second[:, 1:] + bias[None, None, :, None, :]


def slc_importance(p, n_sel):
    n_c = p.shape[-1]
    pad_cfg = [(0, 0)] * (p.ndim - 1) + [(1, SEL_RATIO * (n_sel + 1) - n_c - 1)]
    pp = jnp.pad(p, pad_cfg)
    out = 0.0
    for r, wgt in enumerate(SLC_W):
        out = out + wgt * pp[..., r:r + SEL_RATIO * n_sel:SEL_RATIO]
    return out


def nsa_core(q, q_rot, t, kv_c, cmp_end, fetch, n_sel, kv_win, s_win, gates):
    B, Q = q.shape[:2]
    scale = N_HEAD_DIM ** -0.5
    s_c = jnp.einsum('bqhgd,bnhd->bqhgn', q, kv_c[:, :, 0]) * scale
    p_c = masked_softmax(s_c, (cmp_end[None, :] <= t[:, None])[None, :, None, None, :])
    o_c = jnp.einsum('bqhgn,bnhd->bqhgd', p_c.astype(q.dtype), kv_c[:, :, 1])
    imp = slc_importance(jnp.sum(p_c, axis=3), n_sel)
    j = jnp.arange(n_sel, dtype=jnp.int32)[None, :]
    cur = (t // L_SEL)[:, None]
    forced = (j == 0) | (j == cur) | (j == cur - 1)
    adj = jnp.where(j > cur, NEG, jnp.where(forced, FORCE, 0.0))
    _, idx = lax.top_k(imp + adj[None, :, None, :], min(TOP_N, n_sel))
    kvs = fetch(idx)
    n_k = idx.shape[-1] * L_SEL
    ks = kvs[..., 0, :].reshape(B, Q, KV_HEADS, n_k, N_HEAD_DIM)
    vs = kvs[..., 1, :].reshape(B, Q, KV_HEADS, n_k, N_HEAD_DIM)
    pos_s = (idx[..., None] * L_SEL + jnp.arange(L_SEL, dtype=jnp.int32)).reshape(B, Q, KV_HEADS, n_k)
    s_s = jnp.einsum('bqhgd,bqhsd->bqhgs', q_rot, ks) * scale
    p_s = masked_softmax(s_s, (pos_s <= t[None, :, None, None])[:, :, :, None, :])
    o_s = jnp.einsum('bqhgs,bqhsd->bqhgd', p_s.astype(q.dtype), vs)
    dt = t[:, None] - s_win[None, :]
    mask_w = (dt >= 0) & (dt < WINDOW) & (s_win[None, :] >= 0)
    s_w = jnp.einsum('bqhgd,bshd->bqhgs', q_rot, kv_win[:, :, 0]) * scale
    p_w = masked_softmax(s_w, mask_w[None, :, None, None, :])
    o_w = jnp.einsum('bqhgs,bshd->bqhgd', p_w.astype(q.dtype), kv_win[:, :, 1])
    return gates[..., 0:1] * o_c + gates[..., 1:2] * o_s + gates[..., 2:3] * o_w


def nsa_prompt(q, q_rot, kvc, kvs, kvw, gates, pos, w_cmp, pe_cmp):
    B, T = q.shape[:2]
    first, second = cmp_sub(kvc, w_cmp)
    kv_c = compress_blocks(first, second, w_cmp, pe_cmp)
    cmp_end = D_CMP * jnp.arange(kv_c.shape[1], dtype=jnp.int32) + (L_CMP - 1)
    n_sel = -(-T // L_SEL)
    sel_blocks = jnp.pad(kvs, ((0, 0), (0, n_sel * L_SEL - T), (0, 0), (0, 0), (0, 0)))
    sel_blocks = sel_blocks.reshape(B, n_sel, L_SEL, 2, KV_HEADS, N_HEAD_DIM)
    bi = jnp.arange(B)[:, None, None, None]
    hi = jnp.arange(KV_HEADS)[None, None, :, None]

    def fetch(idx):
        return sel_blocks[bi, idx, :, :, hi, :]

    kvw_pad = jnp.pad(kvw, ((0, 0), (WINDOW, 0), (0, 0), (0, 0), (0, 0)))

    def block(blk):
        start = blk * Q_BLOCK
        t = start + jnp.arange(Q_BLOCK, dtype=jnp.int32)
        cut = lambda a: lax.dynamic_slice_in_dim(a, start, Q_BLOCK, axis=1)
        kv_win = lax.dynamic_slice_in_dim(kvw_pad, start, WINDOW + Q_BLOCK, axis=1)
        s_win = start - WINDOW + jnp.arange(WINDOW + Q_BLOCK, dtype=jnp.int32)
        return nsa_core(cut(q), cut(q_rot), t, kv_c, cmp_end, fetch, n_sel, kv_win, s_win, cut(gates))

    o = lax.map(block, jnp.arange(T // Q_BLOCK, dtype=jnp.int32))
    o = jnp.moveaxis(o, 0, 1).reshape(B, T, D_N)
    return o, kvw[:, -min(WINDOW, T):]


def nsa_sample(q, q_rot, kvc, kvs, kvw, gates, pos, cache_kv_cmp, cache_kv_sel, win_buf, page_table, layer, w_cmp, pe_cmp):
    B, T = q.shape[:2]
    past = page_table.shape[1] * PAGE_SIZE
    past_cmp = cache_kv_cmp[layer, page_table].reshape(B, past, 2, KV_HEADS, N_HEAD_DIM)
    new_cmp = jnp.pad(kvc, ((0, 0), (0, (-T) % D_CMP), (0, 0), (0, 0), (0, 0)))
    f_p, s_p = cmp_sub(past_cmp, w_cmp)
    f_n, s_n = cmp_sub(new_cmp, w_cmp)
    kv_c = compress_blocks(jnp.concatenate([f_p, f_n], axis=1), jnp.concatenate([s_p, s_n], axis=1), w_cmp, pe_cmp)
    cmp_end = D_CMP * jnp.arange(kv_c.shape[1], dtype=jnp.int32) + (L_CMP - 1)
    n_pb = past // L_SEL
    n_nb = -(-T // L_SEL)
    n_sel = n_pb + n_nb
    bpp = PAGE_SIZE // L_SEL
    pool = cache_kv_sel.reshape(cache_kv_sel.shape[:2] + (bpp, L_SEL, 2, KV_HEADS, N_HEAD_DIM))
    new_blocks = jnp.pad(kvs, ((0, 0), (0, n_nb * L_SEL - T), (0, 0), (0, 0), (0, 0)))
    new_blocks = new_blocks.reshape(B, n_nb, L_SEL, 2, KV_HEADS, N_HEAD_DIM)
    bi = jnp.arange(B)[:, None, None, None]
    hi = jnp.arange(KV_HEADS)[None, None, :, None]

    def fetch(idx):
        jp = jnp.minimum(idx, n_pb - 1)
        pages = page_table[bi, jp // bpp]
        old = pool[layer, pages, jp % bpp, :, :, hi, :]
        jn = jnp.clip(idx - n_pb, 0, n_nb - 1)
        new = new_blocks[bi, jn, :, :, hi, :]
        return jnp.where((idx >= n_pb)[..., None, None, None], new, old)

    kv_win = jnp.concatenate([win_buf.astype(kvw.dtype), kvw], axis=1)
    n_buf = win_buf.shape[1]
    s_win = past - n_buf + jnp.arange(n_buf + T, dtype=jnp.int32)
    o = nsa_core(q, q_rot, pos, kv_c, cmp_end, fetch, n_sel, kv_win, s_win, gates)
    return o.reshape(B, T, D_N), kv_win[:, -min(WINDOW, past + T):]


def project(x, g_pre, w_in, pos):
    B, T, _ = x.shape
    u = rmsnorm(x, g_pre) @ w_in
    cuts = [int(c) for c in np.cumsum(SPLITS)[:-1]]
    q_m, k_m, v_m, o_m, z_m, i_m, f_m, q_n, kvc, kvs, kvw, g_n, z_n = jnp.split(u, cuts, axis=-1)
    kv_shape = (B, T, 2, KV_HEADS, N_HEAD_DIM)
    kvc = kvc.reshape(kv_shape)
    kvs = rope_kv(kvs.reshape(kv_shape), pos)
    kvw = rope_kv(kvw.reshape(kv_shape), pos)
    qh = q_n.reshape(B, T, N_HEADS, N_HEAD_DIM)
    q_rot = rope(qh, pos).reshape(B, T, KV_HEADS, GQA, N_HEAD_DIM)
    q = qh.reshape(B, T, KV_HEADS, GQA, N_HEAD_DIM)
    gates = jax.nn.sigmoid(g_n).reshape(B, T, KV_HEADS, GQA, 3)
    return (q_m, k_m, v_m, o_m, z_m, i_m, f_m), (q, q_rot, kvc, kvs, kvw, gates, z_n)


def merge(x, h_m, z_m, o_n, z_n, w_out, g_post):
    y = jnp.concatenate([h_m * jax.nn.silu(z_m), o_n * jax.nn.silu(z_n)], axis=-1) @ w_out
    return x + rmsnorm(y, g_post)


def setup_inputs(seed: int = 0) -> dict:
    key = jax.random.key(seed)
    ks = jax.random.split(key, 24)
    f32 = jnp.float32
    n_pages = PAST_LEN // PAGE_SIZE
    n_used = DEC_BATCH * n_pages
    n_pool = n_used + max(n_used // 4, 1)
    win_buf = min(WINDOW, PAST_LEN)
    kvw = (2, KV_HEADS, N_HEAD_DIM)

    def nrm(k, shape, s=1.0):
        return s * jax.random.normal(k, shape, f32)

    page_table = jax.random.permutation(ks[9], n_pool)[:n_used].reshape(DEC_BATCH, n_pages).astype(jnp.int32)
    return {
        'x_prompt': nrm(ks[0], (BATCH, SEQ, D_MODEL)),
        'x_sample': nrm(ks[1], (DEC_BATCH, DEC_SEQ, D_MODEL)),
        'cache_kv_cmp': nrm(ks[2], (DEPTH, n_pool, PAGE_SIZE) + kvw),
        'cache_kv_sel': nrm(ks[3], (DEPTH, n_pool, PAGE_SIZE) + kvw),
        'cache_kv_win': nrm(ks[4], (DEPTH, DEC_BATCH, win_buf) + kvw),
        'state_conv': nrm(ks[5], (DEPTH, DEC_BATCH, CONV_W - 1, 2 * D_M)),
        'state_C': nrm(ks[6], (DEPTH, DEC_BATCH, M_HEADS, M_HEAD_DIM, M_HEAD_DIM), 0.05),
        'state_n': nrm(ks[7], (DEPTH, DEC_BATCH, M_HEADS, M_HEAD_DIM), 0.2),
        'state_m': nrm(ks[8], (DEPTH, DEC_BATCH, M_HEADS)),
        'page_table': page_table,
        'g_pre': 1.0 + nrm(ks[10], (DEPTH, D_MODEL), 0.05),
        'w_in': nrm(ks[11], (DEPTH, D_MODEL, D_IN), D_MODEL ** -0.5),
        'b_igate': nrm(ks[12], (DEPTH, M_HEADS), 0.1),
        'b_fgate': jnp.linspace(3.0, 6.0, M_HEADS, dtype=f32)[None, :] + nrm(ks[13], (DEPTH, M_HEADS), 0.1),
        'w_conv': nrm(ks[14], (DEPTH, CONV_W, 2 * D_M), CONV_W ** -0.5),
        'b_conv': nrm(ks[15], (DEPTH, 2 * D_M), 0.02),
        'w_cmp': nrm(ks[16], (DEPTH, 2, L_CMP, N_HEAD_DIM, N_HEAD_DIM), (L_CMP * N_HEAD_DIM) ** -0.5),
        'pe_cmp': nrm(ks[17], (DEPTH, 2, L_CMP, N_HEAD_DIM), 0.5),
        'w_out': nrm(ks[18], (DEPTH, D_MIX, D_MODEL), D_MIX ** -0.5),
        'g_post': 1.0 + nrm(ks[19], (DEPTH, D_MODEL), 0.05),
    }


def reference(x_prompt, x_sample, cache_kv_cmp, cache_kv_sel, cache_kv_win, state_conv, state_C, state_n, state_m,
              page_table, g_pre, w_in, b_igate, b_fgate, w_conv, b_conv, w_cmp, pe_cmp, w_out, g_post):
    pos_p = jnp.arange(x_prompt.shape[1], dtype=jnp.int32)
    past = page_table.shape[1] * PAGE_SIZE
    pos_s = past + jnp.arange(x_sample.shape[1], dtype=jnp.int32)
    bp = x_prompt.shape[0]
    xp, xs = x_prompt, x_sample
    p_cmp, p_sel, p_win, p_conv, p_C, p_n, p_m = [], [], [], [], [], [], []
    s_cmp, s_sel, s_win, s_conv, s_C, s_n, s_m = [], [], [], [], [], [], []
    for layer in range(DEPTH):
        (q_m, k_m, v_m, o_m, z_m, i_m, f_m), (q, q_rot, kvc, kvs, kvw, gates, z_n) = project(xp, g_pre[layer], w_in[layer], pos_p)
        h_m, conv_new, C_new, n_new, m_new = mlstm_group(
            q_m, k_m, v_m, o_m, i_m, f_m,
            jnp.zeros((bp, CONV_W - 1, 2 * D_M), xp.dtype),
            jnp.zeros((bp, M_HEADS, M_HEAD_DIM, M_HEAD_DIM), jnp.float32),
            jnp.zeros((bp, M_HEADS, M_HEAD_DIM), jnp.float32),
            jnp.zeros((bp, M_HEADS), jnp.float32),
            w_conv[layer], b_conv[layer], b_igate[layer], b_fgate[layer])
        o_n, win_new = nsa_prompt(q, q_rot, kvc, kvs, kvw, gates, pos_p, w_cmp[layer], pe_cmp[layer])
        xp = merge(xp, h_m, z_m, o_n, z_n, w_out[layer], g_post[layer])
        p_cmp.append(kvc); p_sel.append(kvs); p_win.append(win_new)
        p_conv.append(conv_new); p_C.append(C_new); p_n.append(n_new); p_m.append(m_new)
        (q_m, k_m, v_m, o_m, z_m, i_m, f_m), (q, q_rot, kvc, kvs, kvw, gates, z_n) = project(xs, g_pre[layer], w_in[layer], pos_s)
        h_m, conv_new, C_new, n_new, m_new = mlstm_group(
            q_m, k_m, v_m, o_m, i_m, f_m,
            state_conv[layer], state_C[layer], state_n[layer], state_m[layer],
            w_conv[layer], b_conv[layer], b_igate[layer], b_fgate[layer])
        o_n, win_new = nsa_sample(q, q_rot, kvc, kvs, kvw, gates, pos_s, cache_kv_cmp, cache_kv_sel,
                                  cache_kv_win[layer], page_table, layer, w_cmp[layer], pe_cmp[layer])
        xs = merge(xs, h_m, z_m, o_n, z_n, w_out[layer], g_post[layer])
        s_cmp.append(kvc); s_sel.append(kvs); s_win.append(win_new)
        s_conv.append(conv_new); s_C.append(C_new); s_n.append(n_new); s_m.append(m_new)
    return (xp, xs,
            jnp.stack(p_cmp), jnp.stack(p_sel), jnp.stack(p_win), jnp.stack(p_conv),
            jnp.stack(p_C), jnp.stack(p_n), jnp.stack(p_m),
            jnp.stack(s_cmp), jnp.stack(s_sel), jnp.stack(s_win), jnp.stack(s_conv),
            jnp.stack(s_C), jnp.stack(s_n), jnp.stack(s_m))
```

```python
import functools

import jax
import jax.numpy as jnp
from jax import lax
from jax.experimental import pallas as pl
from jax.experimental.pallas import tpu as pltpu

F32 = jnp.float32
BF16 = jnp.bfloat16
HI = lax.Precision.HIGHEST

D_MODEL = 1024
M_HEADS = 4
M_HEAD_DIM = 128
D_M = M_HEADS * M_HEAD_DIM
CONV_W = 4
N_HEADS = 8
N_HEAD_DIM = 64
D_N = N_HEADS * N_HEAD_DIM
KV_HEADS = 2
GQA = N_HEADS // KV_HEADS
D_CMP = 16
L_CMP = 32
L_SEL = 64
TOP_N = 16
WINDOW = 512
ROT_HALF = 8
ROPE_THETA = 500000.0
NORM_EPS = 1e-6
NEG = -1e30
FORCE = 1e4
KV_W = 2 * KV_HEADS * N_HEAD_DIM
PAGE_SIZE = 128
SLC_W = (1.0, 2.0, 2.0, 2.0, 1.0)
SEL_RATIO = L_SEL // D_CMP
QK_SCALE = N_HEAD_DIM ** -0.5

V7X_VMEM_BYTES = 64 * 1024 * 1024
VMEM_LIMIT = V7X_VMEM_BYTES - 8 * 1024 * 1024
LANES = 128
SUBLANES = 8

PROJ_ROWS = 256
Q_BLOCK = 128
KEY_TILE = 512
MLSTM_CHUNK = 256
CMP_COLS = D_CMP * KV_W


def _dot(a, b, prec=None):
    return jnp.dot(a, b, preferred_element_type=F32, precision=prec)


def _dot_nt(a, b, prec=None):
    return lax.dot_general(a, b, (((1,), (1,)), ((), ())), preferred_element_type=F32, precision=prec)


def _dot_tn(a, b, prec=None):
    return lax.dot_general(a, b, (((0,), (0,)), ((), ())), preferred_element_type=F32, precision=prec)


def _log_sigmoid(x):
    return jnp.minimum(x, 0.0) - jnp.log1p(jnp.exp(-jnp.abs(x)))


def _iota(shape, dim):
    return lax.broadcasted_iota(jnp.int32, shape, dim)


def _const_spec(a):
    nd = a.ndim
    return pl.BlockSpec(a.shape, lambda *_: (0,) * nd)


def _params(sem):
    return pltpu.CompilerParams(dimension_semantics=sem, vmem_limit_bytes=VMEM_LIMIT)


def _proj_kernel(x_ref, g_ref, wlp_ref, whp_ref, wms_ref, ra_ref, rb_ref, rc_ref,
                 qk_ref, v_ref, o_ref, zm_ref, q_ref, qr_ref, kvc_ref, kvs_ref, kvw_ref, zn_ref, ms_ref):
    x = x_ref[...]
    xn = x * lax.rsqrt(jnp.mean(x * x, axis=-1, keepdims=True) + NORM_EPS) * g_ref[...]
    xb = xn.astype(BF16)
    ra, rb, rc = ra_ref[...], rb_ref[...], rc_ref[...]

    def rope(a):
        return a * ra + pltpu.roll(a, LANES - ROT_HALF, 1) * rb + pltpu.roll(a, ROT_HALF, 1) * rc

    def lp(lo, hi):
        return jnp.dot(xb, wlp_ref[:, lo:hi], preferred_element_type=F32)

    qk_ref[...] = lp(0, 1024)
    v_ref[...] = lp(1024, 1536)
    o_ref[...] = lp(1536, 2048)
    zm_ref[...] = lp(2048, 2560)
    kvs = lp(2560, 2816)
    kvs_ref[:, :LANES] = rope(kvs[:, :LANES])
    kvs_ref[:, LANES:] = kvs[:, LANES:]
    kvw = lp(2816, 3072)
    kvw_ref[:, :LANES] = rope(kvw[:, :LANES])
    kvw_ref[:, LANES:] = kvw[:, LANES:]
    zn_ref[...] = lp(3072, 3584)
    hp = jnp.dot(xn, whp_ref[...], preferred_element_type=F32, precision=HI)
    q_ref[...] = hp[:, :D_N]
    for c in range(D_N // LANES):
        qr_ref[:, c * LANES:(c + 1) * LANES] = rope(hp[:, c * LANES:(c + 1) * LANES])
    kvc_ref[...] = hp[:, D_N:]
    ms_ref[...] = jnp.dot(xn, wms_ref[...], preferred_element_type=F32, precision=HI)


def _rope_tables(pos):
    inv = ROPE_THETA ** (-jnp.arange(ROT_HALF, dtype=F32) / ROT_HALF)
    ang = pos.astype(F32)[:, None] * inv[None, :]
    cos, sin = jnp.cos(ang), jnp.sin(ang)
    one = jnp.ones((pos.shape[0], N_HEAD_DIM - 2 * ROT_HALF), F32)
    zero8 = jnp.zeros_like(cos)
    zero = jnp.zeros_like(one)
    ra = jnp.concatenate([cos, cos, one], axis=1)
    rb = jnp.concatenate([-sin, zero8, zero], axis=1)
    rc = jnp.concatenate([zero8, sin, zero], axis=1)
    return tuple(jnp.tile(t, (1, LANES // N_HEAD_DIM)) for t in (ra, rb, rc))


def _project(x2d, g_pre, w_lp, w_hp, w_ms, tabs, tab_blocks):
    n = x2d.shape[0]
    r = PROJ_ROWS
    row = lambda w: pl.BlockSpec((r, w), lambda i: (i, 0))
    tab = pl.BlockSpec((r, LANES), lambda i: (i % tab_blocks, 0))
    widths = (1024, 512, 512, 512, 512, 512, 256, 256, 256, 512, 128)
    return pl.pallas_call(
        _proj_kernel,
        grid=(n // r,),
        in_specs=[row(D_MODEL), _const_spec(g_pre), _const_spec(w_lp), _const_spec(w_hp), _const_spec(w_ms), tab, tab, tab],
        out_specs=[row(w) for w in widths],
        out_shape=[jax.ShapeDtypeStruct((n, w), F32) for w in widths],
        compiler_params=_params(("parallel",)),
        name="project",
    )(x2d, g_pre, w_lp, w_hp, w_ms, *tabs)


def _mlstm_kernel(qk_ref, v_ref, ms_ref, mst_ref, tail_ref, c0_ref, n0_ref, m0_ref, wc_ref, bc_ref, gbr_ref, gbc_ref,
                  h_ref, c_ref, n_ref, m_ref, xp_sc, c_sc, n_sc, m_sc, *, chunk, n_valid):
    L = chunk
    ci = pl.program_id(1)

    @pl.when(ci == 0)
    def _():
        xp_sc[0:SUBLANES, :] = tail_ref[0]
        c_sc[...] = c0_ref[0]
        n_sc[0:M_HEADS, :] = n0_ref[0]
        m_sc[0:M_HEADS, :] = m0_ref[0]

    xp_sc[SUBLANES:SUBLANES + L, :] = qk_ref[0]
    y = bc_ref[...]
    for j in range(CONV_W):
        y = y + xp_sc[pl.ds(SUBLANES - (CONV_W - 1) + j, L), :] * wc_ref[j:j + 1, :]
    qk = y * jax.nn.sigmoid(y)
    xp_sc[0:SUBLANES, :] = xp_sc[L:L + SUBLANES, :]

    row = _iota((L, L), 0)
    col = _iota((L, L), 1)
    tri = row >= col
    g = ms_ref[0] + gbr_ref[...]
    fl = _log_sigmoid(g)
    g_t = mst_ref[0] + gbc_ref[...]
    fl_t = _log_sigmoid(g_t)
    if n_valid < L:
        ok = _iota((L, LANES), 0) < n_valid
        g = jnp.where(ok, g, NEG)
        fl = jnp.where(ok, fl, 0.0)
        ok_t = _iota((SUBLANES, L), 1) < n_valid
        g_t = jnp.where(ok_t, g_t, NEG)
        fl_t = jnp.where(ok_t, fl_t, 0.0)
    cum = _dot(jnp.where(tri, 1.0, 0.0), fl, HI)
    cum_t = _dot(fl_t, jnp.where(row <= col, 1.0, 0.0), HI)

    for h in range(M_HEADS):
        hs = slice(h * M_HEAD_DIM, (h + 1) * M_HEAD_DIM)
        q = qk[:, hs]
        k = qk[:, D_M + h * M_HEAD_DIM:D_M + (h + 1) * M_HEAD_DIM] * (M_HEAD_DIM ** -0.5)
        v = v_ref[0, :, hs]
        ic = g[:, h:h + 1]
        bcol = cum[:, M_HEADS + h:M_HEADS + h + 1]
        ir = g_t[h:h + 1, :]
        brow = cum_t[M_HEADS + h:M_HEADS + h + 1, :]
        m_prev = m_sc[h:h + 1, 0:1]
        c_old = c_sc[h]
        n_old = n_sc[h:h + 1, :]
        log_d = jnp.where(tri, bcol - brow + ir, NEG)
        m_inter = bcol + m_prev
        m_new = jnp.maximum(m_inter, jnp.max(log_d, axis=1, keepdims=True))
        w = _dot_nt(q, k, HI) * jnp.exp(log_d - m_new)
        inter = jnp.exp(m_inter - m_new)
        num = _dot(w, v, HI) + inter * _dot_nt(q, c_old, HI)
        nq = jnp.sum(w, axis=1, keepdims=True) + inter * jnp.sum(q * n_old, axis=1, keepdims=True)
        h_ref[0, :, hs] = num / jnp.maximum(jnp.abs(nq), jnp.exp(-m_new))
        m_end = m_new[L - 1:L, :]
        b_end = bcol[L - 1:L, :]
        decay = jnp.exp(b_end + m_prev - m_end)
        wk = jnp.exp(b_end - bcol + ic - m_end)
        c_sc[h] = decay * c_old + _dot_tn(wk * v, k, HI)
        n_sc[h:h + 1, :] = decay * n_old + jnp.sum(wk * k, axis=0, keepdims=True)
        m_sc[h:h + 1, :] = jnp.broadcast_to(m_end, (1, LANES))

    @pl.when(ci == pl.num_programs(1) - 1)
    def _():
        c_ref[0] = c_sc[...]
        n_ref[0] = n_sc[0:M_HEADS, :]
        m_ref[0] = m_sc[0:M_HEADS, :]


def _mlstm(qk, v, ms, tail, c0, n0, m0, w_conv, b_conv, b_igate, b_fgate, chunk, n_valid):
    b, t, _ = qk.shape
    nc = t // chunk
    gb = jnp.concatenate([b_igate, b_fgate])
    gb_row = jnp.zeros((1, LANES), F32).at[0, :2 * M_HEADS].set(gb)
    gb_col = gb.reshape(2 * M_HEADS, 1)
    ms_t = jnp.swapaxes(ms[:, :, :2 * M_HEADS], 1, 2)
    seq = lambda w: pl.BlockSpec((1, chunk, w), lambda i, c: (i, c, 0))
    per_b = lambda a: pl.BlockSpec((1,) + a.shape[1:], lambda i, c: (i,) + (0,) * (a.ndim - 1))
    kern = functools.partial(_mlstm_kernel, chunk=chunk, n_valid=n_valid)
    out_shape = [jax.ShapeDtypeStruct((b, t, D_M), F32),
                 jax.ShapeDtypeStruct((b, M_HEADS, M_HEAD_DIM, M_HEAD_DIM), F32),
                 jax.ShapeDtypeStruct((b, M_HEADS, M_HEAD_DIM), F32),
                 jax.ShapeDtypeStruct((b, M_HEADS, LANES), F32)]
    return pl.pallas_call(
        kern,
        grid=(b, nc),
        in_specs=[seq(2 * D_M), seq(D_M), seq(LANES),
                  pl.BlockSpec((1, 2 * M_HEADS, chunk), lambda i, c: (i, 0, c)),
                  per_b(tail), per_b(c0), per_b(n0), per_b(m0),
                  _const_spec(w_conv), _const_spec(b_conv.reshape(1, -1)), _const_spec(gb_row), _const_spec(gb_col)],
        out_specs=[seq(D_M), per_b(out_shape[1]), per_b(out_shape[2]), per_b(out_shape[3])],
        out_shape=out_shape,
        scratch_shapes=[pltpu.VMEM((chunk + SUBLANES, 2 * D_M), F32),
                        pltpu.VMEM((M_HEADS, M_HEAD_DIM, M_HEAD_DIM), F32),
                        pltpu.VMEM((SUBLANES, M_HEAD_DIM), F32),
                        pltpu.VMEM((SUBLANES, LANES), F32)],
        compiler_params=_params(("parallel", "arbitrary")),
        name="mlstm",
    )(qk, v, ms, ms_t, tail, c0, n0, m0, w_conv, b_conv.reshape(1, -1), gb_row, gb_col)


def _sub_block_projections(x_of_p, w_ref):
    acc = _dot(x_of_p(0), w_ref[0], HI)
    for p in range(1, D_CMP):
        acc = acc + _dot(x_of_p(p), w_ref[p], HI)
    return acc


def _compress_bias(pe_ref, w_ref):
    b8 = _sub_block_projections(lambda p: pe_ref[p], w_ref)
    return b8[0:1, 0:KV_W] + b8[1:2, KV_W:2 * KV_W]


def _combine_sub_blocks(fs_sc, n_sub, bias):
    return fs_sc[0:n_sub, 0:KV_W] + fs_sc[pl.ds(1, n_sub), KV_W:2 * KV_W] + bias


def _compress_kernel(x_ref, w_ref, pe_ref, o_ref, fs_sc):
    n_sub = x_ref.shape[1]
    fs_sc[0:n_sub, :] = _sub_block_projections(lambda p: x_ref[0, :, p * KV_W:(p + 1) * KV_W], w_ref)
    fs_sc[n_sub:n_sub + SUBLANES, :] = jnp.zeros((SUBLANES, 2 * KV_W), F32)
    o_ref[0] = _combine_sub_blocks(fs_sc, n_sub, _compress_bias(pe_ref, w_ref))


def _compress(x, w_big, pe_big):
    b, n_sub, _ = x.shape
    return pl.pallas_call(
        _compress_kernel,
        grid=(b,),
        in_specs=[pl.BlockSpec((1, n_sub, CMP_COLS), lambda i: (i, 0, 0)), _const_spec(w_big), _const_spec(pe_big)],
        out_specs=pl.BlockSpec((1, n_sub, KV_W), lambda i: (i, 0, 0)),
        out_shape=jax.ShapeDtypeStruct((b, n_sub, KV_W), F32),
        scratch_shapes=[pltpu.VMEM((n_sub + SUBLANES, 2 * KV_W), F32)],
        compiler_params=_params(("parallel",)),
        name="compress",
    )(x, w_big, pe_big)


def _sub_proj_kernel(x_ref, w_ref, o_ref):
    o_ref[...] = _sub_block_projections(lambda p: x_ref[:, p * KV_W:(p + 1) * KV_W], w_ref)


def _sub_proj(x, w_big):
    return pl.pallas_call(
        _sub_proj_kernel,
        out_shape=jax.ShapeDtypeStruct((x.shape[0], 2 * KV_W), F32),
        compiler_params=pltpu.CompilerParams(vmem_limit_bytes=VMEM_LIMIT),
        name="sub_proj_new",
    )(x, w_big)


def _compress_weights(w_cmp, pe_cmp):
    w5 = w_cmp.reshape(2, 2, D_CMP, N_HEAD_DIM, N_HEAD_DIM)
    eye = jnp.eye(KV_HEADS, dtype=F32)
    eyec = jnp.eye(2, dtype=F32)
    w_big = jnp.einsum('cfpde,cC,hH->pchdfCHe', w5, eyec, eye).reshape(D_CMP, KV_W, 2 * KV_W)
    pe4 = pe_cmp.reshape(2, 2, D_CMP, N_HEAD_DIM)
    pe_rows = jnp.broadcast_to(pe4.transpose(2, 1, 0, 3)[:, :, :, None, :],
                               (D_CMP, 2, 2, KV_HEADS, N_HEAD_DIM)).reshape(D_CMP, 2, KV_W)
    pe_big = jnp.zeros((D_CMP, SUBLANES, KV_W), F32).at[:, 0:2, :].set(pe_rows)
    return w_big, pe_big


def _importance_matrix(n_sel_pad, n_cmp_pad):
    j = jnp.arange(n_sel_pad)[:, None]
    n = jnp.arange(n_cmp_pad)[None, :]
    r = n - SEL_RATIO * j + 1
    w = jnp.asarray(SLC_W, F32)
    return jnp.where((r >= 0) & (r < len(SLC_W)), w[jnp.clip(r, 0, len(SLC_W) - 1)], 0.0).astype(F32)


def _rank_select(v, n_cand):
    j = _iota(v.shape, 0)
    cnt = jnp.zeros(v.shape, F32)
    for jp in range(n_cand):
        rowv = v[jp:jp + 1, :]
        cnt = cnt + jnp.where(j > jp, jnp.where(rowv >= v, 1.0, 0.0), jnp.where(rowv > v, 1.0, 0.0))
    return jnp.where(cnt < TOP_N, 1.0, 0.0)


def _selection_bonus(j, t):
    cur = t // L_SEL
    forced = (j == 0) | (j == cur) | (j == cur - 1)
    return jnp.where(j > cur, NEG, jnp.where(forced, FORCE, 0.0))


def _nsa_prompt_kernel(q_ref, qr_ref, g_ref, kc_ref, vct_ref, mt_ref, ks_ref, vs_ref, kw_ref, vw_ref, e_ref,
                       o_ref, m_sc, l_sc, acc_sc):
    qb = pl.program_id(2)
    start = qb * Q_BLOCK
    rows = GQA * Q_BLOCK
    n_cmp = kc_ref.shape[2]
    n_sel = mt_ref.shape[0]

    q = q_ref[0, 0, 0] * QK_SCALE
    s_t = _dot_nt(kc_ref[0, 0], q, HI)
    n_i = _iota((n_cmp, rows), 0)
    t_l = start + (_iota((n_cmp, rows), 1) & (Q_BLOCK - 1))
    ok = (D_CMP * n_i + (L_CMP - 1)) <= t_l
    s_t = jnp.where(ok, s_t, NEG)
    p_t = jnp.where(ok, jnp.exp(s_t - jnp.max(s_t, axis=0, keepdims=True)), 0.0)
    l_t = jnp.sum(p_t, axis=0, keepdims=True)
    p_t = p_t / jnp.where(l_t > 0.0, l_t, 1.0)
    o_c = _dot(vct_ref[0, 0].astype(BF16), p_t.astype(BF16)).T

    p_sum = p_t[:, 0:Q_BLOCK]
    for gi in range(1, GQA):
        p_sum = p_sum + p_t[:, gi * Q_BLOCK:(gi + 1) * Q_BLOCK]
    imp_t = _dot(mt_ref[...], p_sum, HI)
    j = _iota((n_sel, Q_BLOCK), 0)
    t_q = start + _iota((n_sel, Q_BLOCK), 1)
    sel = _rank_select(imp_t + _selection_bonus(j, t_q), n_sel).T.astype(BF16)

    qr = (qr_ref[0, 0, 0] * QK_SCALE).astype(BF16)
    m_sc[...] = jnp.full(m_sc.shape, NEG, F32)
    l_sc[...] = jnp.zeros(l_sc.shape, F32)
    acc_sc[...] = jnp.zeros(acc_sc.shape, F32)
    t_r = start + (_iota((rows, KEY_TILE), 0) & (Q_BLOCK - 1))
    lane_k = _iota((rows, KEY_TILE), 1)

    def tile(kt, carry):
        k0 = pl.multiple_of(kt * KEY_TILE, KEY_TILE)
        s = _dot_nt(qr, ks_ref[0, 0, pl.ds(k0, KEY_TILE), :])
        picked = _dot(sel, e_ref[:, pl.ds(k0, KEY_TILE)])
        picked = jnp.concatenate([picked] * GQA, axis=0)
        ok_s = (picked > 0.5) & ((k0 + lane_k) <= t_r)
        s = jnp.where(ok_s, s, NEG)
        m_old = m_sc[...]
        m_new = jnp.maximum(m_old, jnp.max(s, axis=1, keepdims=True))
        p = jnp.where(ok_s, jnp.exp(s - m_new), 0.0)
        alpha = jnp.exp(m_old - m_new)
        l_sc[...] = alpha * l_sc[...] + jnp.sum(p, axis=1, keepdims=True)
        acc_sc[...] = alpha * acc_sc[...] + _dot(p.astype(BF16), vs_ref[0, 0, pl.ds(k0, KEY_TILE), :])
        m_sc[...] = m_new
        return carry

    lax.fori_loop(0, (start + Q_BLOCK + KEY_TILE - 1) // KEY_TILE, tile, 0)
    o_s = acc_sc[...] / l_sc[...]

    span = WINDOW + Q_BLOCK
    w0 = pl.multiple_of(jnp.maximum(start - WINDOW, 0), Q_BLOCK)
    s = _dot_nt(qr, kw_ref[0, 0, pl.ds(w0, span), :])
    dt = (start + (_iota((rows, span), 0) & (Q_BLOCK - 1))) - (w0 + _iota((rows, span), 1))
    ok_w = (dt >= 0) & (dt < WINDOW)
    s = jnp.where(ok_w, s, NEG)
    p = jnp.where(ok_w, jnp.exp(s - jnp.max(s, axis=1, keepdims=True)), 0.0)
    o_w = _dot(p.astype(BF16), vw_ref[0, 0, pl.ds(w0, span), :]) / jnp.sum(p, axis=1, keepdims=True)

    gate = jax.nn.sigmoid(g_ref[0, 0, 0])
    o_ref[0, 0, 0] = gate[:, 0:1] * o_c + gate[:, 1:2] * o_s + gate[:, 2:3] * o_w


def _nsa_prompt(q_r, qrot_r, g_r, k_c, v_ct, m_t, k_s, v_s, k_w, v_w, expand):
    b, _, nqb, rows, _ = q_r.shape
    t = k_s.shape[2]
    blk = lambda w: pl.BlockSpec((1, 1, 1, rows, w), lambda i, h, c: (i, h, c, 0, 0))
    per_bh = lambda a: pl.BlockSpec((1, 1) + a.shape[2:], lambda i, h, c: (i, h, 0, 0))
    return pl.pallas_call(
        _nsa_prompt_kernel,
        grid=(b, KV_HEADS, nqb),
        in_specs=[blk(N_HEAD_DIM), blk(N_HEAD_DIM), blk(3), per_bh(k_c), per_bh(v_ct), _const_spec(m_t),
                  per_bh(k_s), per_bh(v_s), per_bh(k_w), per_bh(v_w), _const_spec(expand)],
        out_specs=blk(N_HEAD_DIM),
        out_shape=jax.ShapeDtypeStruct((b, KV_HEADS, nqb, rows, N_HEAD_DIM), F32),
        scratch_shapes=[pltpu.VMEM((rows, 1), F32), pltpu.VMEM((rows, 1), F32), pltpu.VMEM((rows, N_HEAD_DIM), F32)],
        compiler_params=_params(("parallel", "parallel", "arbitrary")),
        name="nsa_prompt",
    )(q_r, qrot_r, g_r, k_c, v_ct, m_t, k_s, v_s, k_w, v_w, expand)


def _gather_pages(pt_ref, cache_hbm, buf, sem, rows_per_page, n_pages):
    b = pl.program_id(0)
    slot = b % 2

    def page_copy(seq, j, s):
        return pltpu.make_async_copy(cache_hbm.at[pt_ref[seq, j]],
                                     buf.at[s, pl.ds(j * rows_per_page, rows_per_page)], sem.at[s])

    def issue(seq, s):
        def body(j, c):
            page_copy(seq, j, s).start()
            return c
        lax.fori_loop(0, n_pages, body, 0)

    @pl.when(b == 0)
    def _():
        issue(0, 0)

    @pl.when(b + 1 < pl.num_programs(0))
    def _():
        issue(b + 1, 1 - slot)

    def wait_body(j, c):
        page_copy(b, j, slot).wait()
        return c
    lax.fori_loop(0, n_pages, wait_body, 0)
    return slot


def _nsa_sample_cmp_kernel(pt_ref, cache_hbm, w_ref, pe_ref, fsn_ref, q_ref, mt_ref, gq_ref,
                           oc_ref, sel_ref, xbuf, fs_sc, sem, *, n_pages, past, t_new):
    sub_per_page = PAGE_SIZE // D_CMP
    n_sub = n_pages * sub_per_page
    slot = _gather_pages(pt_ref, cache_hbm, xbuf, sem, sub_per_page, n_pages)

    fs_sc[0:n_sub, :] = _sub_block_projections(lambda p: xbuf[slot, :, p * KV_W:(p + 1) * KV_W], w_ref)
    fs_sc[n_sub:n_sub + SUBLANES, :] = fsn_ref[0]
    kvc = _combine_sub_blocks(fs_sc, n_sub, _compress_bias(pe_ref, w_ref))

    n_sel = mt_ref.shape[0]
    rows = GQA * t_new
    for h in range(KV_HEADS):
        k_h = kvc[:, h * N_HEAD_DIM:(h + 1) * N_HEAD_DIM]
        v_h = kvc[:, KV_W // 2 + h * N_HEAD_DIM:KV_W // 2 + (h + 1) * N_HEAD_DIM]
        q_h = q_ref[0, h * rows:(h + 1) * rows, :] * QK_SCALE
        s_t = _dot_nt(k_h, q_h, HI)
        t_l = past + lax.rem(_iota((n_sub, rows), 1), t_new)
        ok = (D_CMP * _iota((n_sub, rows), 0) + (L_CMP - 1)) <= t_l
        s_t = jnp.where(ok, s_t, NEG)
        p_t = jnp.where(ok, jnp.exp(s_t - jnp.max(s_t, axis=0, keepdims=True)), 0.0)
        l_t = jnp.sum(p_t, axis=0, keepdims=True)
        p_t = p_t / jnp.where(l_t > 0.0, l_t, 1.0)
        imp_t = _dot(mt_ref[...], _dot(p_t, gq_ref[...], HI), HI)
        j = _iota((n_sel, t_new), 0)
        t_q = past + _iota((n_sel, t_new), 1)
        sel_ref[0, h] = _rank_select(imp_t + _selection_bonus(j, t_q), n_sel)
        s = _dot_nt(q_h, k_h, HI)
        t_r = past + lax.rem(_iota((rows, n_sub), 0), t_new)
        ok2 = (D_CMP * _iota((rows, n_sub), 1) + (L_CMP - 1)) <= t_r
        s = jnp.where(ok2, s, NEG)
        p = jnp.where(ok2, jnp.exp(s - jnp.max(s, axis=1, keepdims=True)), 0.0)
        l = jnp.sum(p, axis=1, keepdims=True)
        oc_ref[0, h * rows:(h + 1) * rows, :] = _dot(p.astype(BF16), v_h.astype(BF16)) / jnp.where(l > 0.0, l, 1.0)


def _nsa_sample_cmp(page_table, cache_cmp, w_big, pe_big, fs_new, q_rows, m_t, gq, past, t_new):
    b, n_pages = page_table.shape
    sub_per_page = PAGE_SIZE // D_CMP
    n_sub = n_pages * sub_per_page
    rows = q_rows.shape[1]
    n_sel_pad = m_t.shape[0]
    kern = functools.partial(_nsa_sample_cmp_kernel, n_pages=n_pages, past=past, t_new=t_new)
    const = lambda a: pl.BlockSpec(a.shape, lambda i, pt: (0,) * a.ndim)
    per_b = lambda a: pl.BlockSpec((1,) + a.shape[1:], lambda i, pt: (i,) + (0,) * (a.ndim - 1))
    out_shape = [jax.ShapeDtypeStruct((b, rows, N_HEAD_DIM), F32),
                 jax.ShapeDtypeStruct((b, KV_HEADS, n_sel_pad, t_new), F32)]
    return pl.pallas_call(
        kern,
        grid_spec=pltpu.PrefetchScalarGridSpec(
            num_scalar_prefetch=1,
            grid=(b,),
            in_specs=[pl.BlockSpec(memory_space=pl.ANY), const(w_big), const(pe_big), per_b(fs_new), per_b(q_rows),
                      const(m_t), const(gq)],
            out_specs=[per_b(out_shape[0]), per_b(out_shape[1])],
            scratch_shapes=[pltpu.VMEM((2, n_sub, CMP_COLS), F32),
                            pltpu.VMEM((n_sub + SUBLANES, 2 * KV_W), F32),
                            pltpu.SemaphoreType.DMA((2,))]),
        out_shape=out_shape,
        compiler_params=_params(("arbitrary",)),
        name="nsa_sample_cmp",
    )(page_table, cache_cmp, w_big, pe_big, fs_new, q_rows, m_t, gq)


def _nsa_sample_attn_kernel(pt_ref, cache_hbm, qbd_ref, selt_ref, win_ref, kvsn_ref, kvwn_ref, oc_ref, g_ref,
                            o_ref, kbuf, s_sc, sem, *, n_pages, past, t_new):
    slot = _gather_pages(pt_ref, cache_hbm, kbuf, sem, PAGE_SIZE, n_pages)
    cols = qbd_ref.shape[1]
    half = KV_W // 2
    qbd = (qbd_ref[0] * QK_SCALE).astype(BF16)
    ones = jnp.ones((SUBLANES, LANES), F32)
    n_buf = win_ref.shape[1]
    blocks_per_tile = KEY_TILE // L_SEL
    n_tiles = past // KEY_TILE
    n_pb = past // L_SEL

    def t_of(shape):
        return past + lax.rem(_iota(shape, 1), t_new)

    def new_rows(ref, extra_ok):
        kn = ref[0]
        s = _dot_nt(kn[:, :half].astype(BF16), qbd)
        ki = _iota((SUBLANES, cols), 0)
        ok = (ki < t_new) & ((past + ki) <= t_of((SUBLANES, cols))) & extra_ok
        return jnp.where(ok, s, NEG), ok, kn[:, half:].astype(BF16)

    def finish(acc, l):
        l8 = jnp.where(_iota((SUBLANES, cols), 0) == 0, jnp.broadcast_to(l, (SUBLANES, cols)), 0.0)
        l_col = _dot_tn(l8, ones, HI)[:, :N_HEAD_DIM]
        own = jnp.where(_iota((cols, N_HEAD_DIM), 0) < cols // KV_HEADS, acc[:, :N_HEAD_DIM], acc[:, N_HEAD_DIM:])
        return own / l_col

    def score_tile(c, mx):
        k0 = pl.multiple_of(c * KEY_TILE, KEY_TILE)
        s = _dot_nt(kbuf[slot, pl.ds(k0, KEY_TILE), 0:half].astype(BF16), qbd)
        picked = jnp.concatenate(
            [jnp.broadcast_to(selt_ref[0, pl.ds(c * blocks_per_tile + i, 1), :], (L_SEL, cols))
             for i in range(blocks_per_tile)], axis=0)
        s = jnp.where(picked > 0.5, s, NEG)
        s_sc[pl.ds(k0, KEY_TILE), :] = s
        return jnp.maximum(mx, jnp.max(s, axis=0, keepdims=True))

    mx = lax.fori_loop(0, n_tiles, score_tile, jnp.full((1, cols), NEG, F32))
    s_n, ok_n, v_n = new_rows(kvsn_ref, selt_ref[0, n_pb:n_pb + 1, :] > 0.5)
    mx = jnp.maximum(mx, jnp.max(s_n, axis=0, keepdims=True))
    p_n = jnp.where(ok_n, jnp.exp(s_n - mx), 0.0)

    def value_tile(c, carry):
        l, acc = carry
        k0 = pl.multiple_of(c * KEY_TILE, KEY_TILE)
        s = s_sc[pl.ds(k0, KEY_TILE), :]
        p = jnp.where(s > 0.5 * NEG, jnp.exp(s - mx), 0.0)
        acc = acc + _dot_tn(p.astype(BF16), kbuf[slot, pl.ds(k0, KEY_TILE), half:KV_W].astype(BF16))
        return l + jnp.sum(p, axis=0, keepdims=True), acc

    l, acc = lax.fori_loop(0, n_tiles, value_tile,
                           (jnp.sum(p_n, axis=0, keepdims=True), _dot_tn(p_n.astype(BF16), v_n)))
    o_s = finish(acc, l)

    win = win_ref[0]
    s = _dot_nt(win[:, :half].astype(BF16), qbd)
    dt = t_of((n_buf, cols)) - (past - n_buf + _iota((n_buf, cols), 0))
    ok_w = (dt >= 0) & (dt < WINDOW)
    s = jnp.where(ok_w, s, NEG)
    s_n, ok_n, v_n = new_rows(kvwn_ref, True)
    mx = jnp.maximum(jnp.max(s, axis=0, keepdims=True), jnp.max(s_n, axis=0, keepdims=True))
    p = jnp.where(ok_w, jnp.exp(s - mx), 0.0)
    p_n = jnp.where(ok_n, jnp.exp(s_n - mx), 0.0)
    acc = _dot_tn(p.astype(BF16), win[:, half:].astype(BF16)) + _dot_tn(p_n.astype(BF16), v_n)
    o_w = finish(acc, jnp.sum(p, axis=0, keepdims=True) + jnp.sum(p_n, axis=0, keepdims=True))

    gate = jax.nn.sigmoid(g_ref[0])
    o_ref[0] = gate[:, 0:1] * oc_ref[0] + gate[:, 1:2] * o_s + gate[:, 2:3] * o_w


def _nsa_sample_attn(page_table, cache_sel, qbd, sel_t, win, kvs_new, kvw_new, o_c, g_rows, past, t_new):
    b, n_pages = page_table.shape
    cols = qbd.shape[1]
    kern = functools.partial(_nsa_sample_attn_kernel, n_pages=n_pages, past=past, t_new=t_new)
    per_b = lambda a: pl.BlockSpec((1,) + a.shape[1:], lambda i, pt: (i,) + (0,) * (a.ndim - 1))
    return pl.pallas_call(
        kern,
        grid_spec=pltpu.PrefetchScalarGridSpec(
            num_scalar_prefetch=1,
            grid=(b,),
            in_specs=[pl.BlockSpec(memory_space=pl.ANY), per_b(qbd), per_b(sel_t), per_b(win), per_b(kvs_new),
                      per_b(kvw_new), per_b(o_c), per_b(g_rows)],
            out_specs=per_b(o_c),
            scratch_shapes=[pltpu.VMEM((2, past, KV_W), F32),
                            pltpu.VMEM((past, cols), F32),
                            pltpu.SemaphoreType.DMA((2,))]),
        out_shape=jax.ShapeDtypeStruct(o_c.shape, F32),
        compiler_params=_params(("arbitrary",)),
        name="nsa_sample_attn",
    )(page_table, cache_sel, qbd, sel_t, win, kvs_new, kvw_new, o_c, g_rows)


def _merge_kernel(x_ref, h_ref, om_ref, zm_ref, on_ref, zn_ref, w_ref, g_ref, y_ref):
    zm = zm_ref[...]
    zn = zn_ref[...]
    a = h_ref[...] * jax.nn.sigmoid(om_ref[...]) * (zm * jax.nn.sigmoid(zm))
    c = on_ref[...] * (zn * jax.nn.sigmoid(zn))
    y = _dot(a.astype(BF16), w_ref[0:D_M, :]) + _dot(c.astype(BF16), w_ref[D_M:, :])
    yn = y * lax.rsqrt(jnp.mean(y * y, axis=-1, keepdims=True) + NORM_EPS)
    y_ref[...] = x_ref[...] + yn * g_ref[...]


def _merge(x2d, h_m, o_m, z_m, o_n, z_n, w_out, g_post):
    n = x2d.shape[0]
    r = PROJ_ROWS
    row = lambda w: pl.BlockSpec((r, w), lambda i: (i, 0))
    return pl.pallas_call(
        _merge_kernel,
        grid=(n // r,),
        in_specs=[row(D_MODEL), row(D_M), row(D_M), row(D_M), row(D_N), row(D_N), _const_spec(w_out), _const_spec(g_post)],
        out_specs=row(D_MODEL),
        out_shape=jax.ShapeDtypeStruct((n, D_MODEL), F32),
        compiler_params=_params(("parallel",)),
        name="merge",
    )(x2d, h_m, o_m, z_m, o_n, z_n, w_out, g_post)


def _split_heads(kv, c):
    b, t, _ = kv.shape
    return kv.reshape(b, t, 2, KV_HEADS, N_HEAD_DIM)[:, :, c].transpose(0, 2, 1, 3)


def _group_rows(a, width):
    b, t, _ = a.shape
    a = a.reshape(b, t // Q_BLOCK, Q_BLOCK, KV_HEADS, GQA, width)
    return a.transpose(0, 3, 1, 4, 2, 5).reshape(b, KV_HEADS, t // Q_BLOCK, GQA * Q_BLOCK, width)


def _ungroup_rows(o, b, t):
    o = o.reshape(b, KV_HEADS, t // Q_BLOCK, GQA, Q_BLOCK, N_HEAD_DIM)
    return o.transpose(0, 2, 4, 1, 3, 5).reshape(b, t, D_N)


def _sample_rows(a, width):
    b, t, _ = a.shape
    return a.reshape(b, t, KV_HEADS * GQA, width).transpose(0, 2, 1, 3).reshape(b, KV_HEADS * GQA * t, width)


def kernel(x_prompt, x_sample, cache_kv_cmp, cache_kv_sel, cache_kv_win, state_conv, state_C, state_n, state_m,
           page_table, g_pre, w_in, b_igate, b_fgate, w_conv, b_conv, w_cmp, pe_cmp, w_out, g_post):
    assert g_pre.shape[0] == 1, "single-layer trunk"
    bp, t_p, _ = x_prompt.shape
    bs, t_s, _ = x_sample.shape
    n_pages = page_table.shape[1]
    past = n_pages * PAGE_SIZE
    n_pool = cache_kv_cmp.shape[1]
    w = w_in[0]

    w_lp = jnp.concatenate([w[:, 0:2560], w[:, 3336:3848], w[:, 3872:4384]], axis=1).astype(BF16)
    w_hp = w[:, 2568:3336]
    w_ms = jnp.concatenate([w[:, 2560:2568], w[:, 3848:3872], jnp.zeros((D_MODEL, LANES - 32), F32)], axis=1)
    g_pre2 = g_pre[0].reshape(1, D_MODEL)
    w_big, pe_big = _compress_weights(w_cmp[0], pe_cmp[0])
    w_out_b = w_out[0].astype(BF16)
    g_post2 = g_post[0].reshape(1, D_MODEL)

    tabs_p = _rope_tables(jnp.arange(t_p, dtype=jnp.int32))
    xp2 = x_prompt.reshape(bp * t_p, D_MODEL)
    qk, v_m, o_m, z_m, q_n, q_rot, kvc, kvs, kvw, z_n, ms = _project(
        xp2, g_pre2, w_lp, w_hp, w_ms, tabs_p, t_p // PROJ_ROWS)
    r3 = lambda a: a.reshape(bp, t_p, a.shape[-1])
    qk3 = r3(qk)
    zeros = lambda *s: jnp.zeros(s, F32)
    h_m, c_p, n_p, m_p = _mlstm(qk3, r3(v_m), r3(ms), zeros(bp, SUBLANES, 2 * D_M),
                                zeros(bp, M_HEADS, M_HEAD_DIM, M_HEAD_DIM), zeros(bp, M_HEADS, M_HEAD_DIM),
                                zeros(bp, M_HEADS, LANES), w_conv[0], b_conv[0], b_igate[0], b_fgate[0],
                                MLSTM_CHUNK, MLSTM_CHUNK)
    kvc3, kvs3, kvw3 = r3(kvc), r3(kvs), r3(kvw)
    n_sub_p = t_p // D_CMP
    kv_c = _compress(kvc3.reshape(bp, n_sub_p, CMP_COLS), w_big, pe_big)
    n_sel_p = -(-t_p // L_SEL)
    o_rows = _nsa_prompt(
        _group_rows(r3(q_n), N_HEAD_DIM), _group_rows(r3(q_rot), N_HEAD_DIM),
        _group_rows(r3(ms)[:, :, 8:8 + 3 * N_HEADS], 3),
        _split_heads(kv_c, 0), jnp.swapaxes(_split_heads(kv_c, 1), 2, 3),
        _importance_matrix(n_sel_p, n_sub_p),
        _split_heads(kvs3, 0).astype(BF16), _split_heads(kvs3, 1).astype(BF16),
        _split_heads(kvw3, 0).astype(BF16), _split_heads(kvw3, 1).astype(BF16),
        (jnp.arange(n_sel_p)[:, None] == (jnp.arange(t_p) // L_SEL)[None, :]).astype(BF16))
    o_n = _ungroup_rows(o_rows, bp, t_p).reshape(bp * t_p, D_N)
    y_p = _merge(xp2, h_m.reshape(bp * t_p, D_M), o_m, z_m, o_n, z_n, w_out_b, g_post2).reshape(bp, t_p, D_MODEL)
    kv_shape = lambda a, bb, tt: a.reshape(1, bb, tt, 2, KV_HEADS, N_HEAD_DIM)
    p_out = (kv_shape(kvc3, bp, t_p), kv_shape(kvs3, bp, t_p),
             kv_shape(kvw3[:, -min(WINDOW, t_p):], bp, min(WINDOW, t_p)),
             qk3[:, -(CONV_W - 1):][None], c_p[None], n_p[None], m_p[:, :, 0][None])

    pos_s = past + jnp.arange(t_s, dtype=jnp.int32)
    rows_s = bs * t_s
    tabs_s = tuple(jnp.tile(tb, (PROJ_ROWS // t_s, 1)) for tb in _rope_tables(pos_s))
    xs2 = x_sample.reshape(rows_s, D_MODEL)
    qk, v_m, o_m, z_m, q_n, q_rot, kvc, kvs, kvw, z_n, ms = _project(xs2, g_pre2, w_lp, w_hp, w_ms, tabs_s, 1)
    r3 = lambda a: a.reshape(bs, t_s, a.shape[-1])
    pad8 = lambda a: jnp.pad(r3(a), ((0, 0), (0, SUBLANES - t_s), (0, 0)))
    qk3 = r3(qk)
    tail = jnp.pad(state_conv[0], ((0, 0), (SUBLANES - (CONV_W - 1), 0), (0, 0)))
    m0 = jnp.broadcast_to(state_m[0][:, :, None], (bs, M_HEADS, LANES))
    h_m8, c_s, n_s, m_s = _mlstm(pad8(qk), pad8(v_m), pad8(ms), tail, state_C[0], state_n[0], m0,
                                 w_conv[0], b_conv[0], b_igate[0], b_fgate[0], SUBLANES, t_s)
    h_m = h_m8[:, :t_s].reshape(rows_s, D_M)
    conv_s = jnp.concatenate([state_conv[0], qk3], axis=1)[:, -(CONV_W - 1):]

    kvc3, kvs3, kvw3 = r3(kvc), r3(kvs), r3(kvw)
    sub_per_page = PAGE_SIZE // D_CMP
    n_sub_s = n_pages * sub_per_page
    x_new = jnp.pad(kvc3, ((0, 0), (0, (-t_s) % D_CMP), (0, 0))).reshape(bs, CMP_COLS)
    fs_new = jnp.pad(_sub_proj(x_new, w_big)[:, None, :], ((0, 0), (0, SUBLANES - 1), (0, 0)))
    n_sel_s = past // L_SEL + (-(-t_s // L_SEL))
    n_sel_pad = -(-n_sel_s // SUBLANES) * SUBLANES
    m_t = _importance_matrix(n_sel_pad, n_sub_s) * (jnp.arange(n_sel_pad) < n_sel_s)[:, None]
    gq = (jnp.arange(GQA * t_s)[:, None] % t_s == jnp.arange(t_s)[None, :]).astype(F32)
    o_c, sel = _nsa_sample_cmp(page_table, cache_kv_cmp[0].reshape(n_pool, sub_per_page, CMP_COLS), w_big, pe_big,
                               fs_new, _sample_rows(r3(q_n), N_HEAD_DIM), m_t, gq, past, t_s)
    sel_t = jnp.broadcast_to(sel.transpose(0, 2, 1, 3)[:, :, :, None, :],
                             (bs, n_sel_pad, KV_HEADS, GQA, t_s)).reshape(bs, n_sel_pad, KV_HEADS * GQA * t_s)
    qr_rows = _sample_rows(r3(q_rot), N_HEAD_DIM)
    head_of_row = jnp.arange(KV_HEADS * GQA * t_s) // (GQA * t_s)
    qbd = jnp.concatenate([jnp.where((head_of_row == hh)[None, :, None], qr_rows, 0.0) for hh in range(KV_HEADS)],
                          axis=2)
    o_rows = _nsa_sample_attn(page_table, cache_kv_sel[0].reshape(n_pool, PAGE_SIZE, KV_W), qbd, sel_t,
                              cache_kv_win[0].reshape(bs, -1, KV_W), pad8(kvs), pad8(kvw), o_c,
                              _sample_rows(r3(ms)[:, :, 8:8 + 3 * N_HEADS], 3), past, t_s)
    o_n = o_rows.reshape(bs, KV_HEADS * GQA, t_s, N_HEAD_DIM).transpose(0, 2, 1, 3).reshape(rows_s, D_N)
    y_s = _merge(xs2, h_m, o_m, z_m, o_n, z_n, w_out_b, g_post2).reshape(bs, t_s, D_MODEL)
    win_all = jnp.concatenate([cache_kv_win[0].reshape(bs, -1, KV_W), kvw3], axis=1)
    n_win = min(WINDOW, past + t_s)
    s_out = (kv_shape(kvc3, bs, t_s), kv_shape(kvs3, bs, t_s), kv_shape(win_all[:, -n_win:], bs, n_win),
             conv_s[None], c_s[None], n_s[None], m_s[:, :, 0][None])

    return (y_p, y_s) + p_out + s_out
```

```python
import functools

import jax
import jax.numpy as jnp
from jax import lax
from jax.experimental import pallas as pl
from jax.experimental.pallas import tpu as pltpu

F32 = jnp.float32
BF16 = jnp.bfloat16
HI = lax.Precision.HIGHEST

D_MODEL = 1024
M_HEADS = 4
M_HEAD_DIM = 128
D_M = M_HEADS * M_HEAD_DIM
CONV_W = 4
N_HEADS = 8
N_HEAD_DIM = 64
D_N = N_HEADS * N_HEAD_DIM
KV_HEADS = 2
GQA = N_HEADS // KV_HEADS
D_CMP = 16
L_CMP = 32
L_SEL = 64
TOP_N = 16
WINDOW = 512
ROT_HALF = 8
ROPE_THETA = 500000.0
NORM_EPS = 1e-6
NEG = -1e30
FORCE = 1e4
KV_W = 2 * KV_HEADS * N_HEAD_DIM
PAGE_SIZE = 128
SLC_W = (1.0, 2.0, 2.0, 2.0, 1.0)
SEL_RATIO = L_SEL // D_CMP
QK_SCALE = N_HEAD_DIM ** -0.5

V7X_VMEM_BYTES = 64 * 1024 * 1024
VMEM_LIMIT = V7X_VMEM_BYTES - 8 * 1024 * 1024
LANES = 128
SUBLANES = 8

PROJ_ROWS = 256
Q_BLOCK = 128
KEY_TILE = 512
MLSTM_CHUNK = 256
CMP_COLS = D_CMP * KV_W


def _dot(a, b, prec=None):
    return jnp.dot(a, b, preferred_element_type=F32, precision=prec)


def _dot_nt(a, b, prec=None):
    return lax.dot_general(a, b, (((1,), (1,)), ((), ())), preferred_element_type=F32, precision=prec)


def _dot_tn(a, b, prec=None):
    return lax.dot_general(a, b, (((0,), (0,)), ((), ())), preferred_element_type=F32, precision=prec)


def _split_bf16(x):
    hi = x.astype(BF16)
    return hi, (x - hi.astype(F32)).astype(BF16)


def _dot_3pass(x, w_hi, w_lo):
    x_hi, x_lo = _split_bf16(x)
    return _dot(x_hi, w_hi) + _dot(x_lo, w_hi) + _dot(x_hi, w_lo)


def _dot_nt_3pass(a, b):
    a_hi, a_lo = _split_bf16(a)
    b_hi, b_lo = _split_bf16(b)
    return _dot_nt(a_hi, b_hi) + _dot_nt(a_lo, b_hi) + _dot_nt(a_hi, b_lo)


def _log_sigmoid(x):
    return jnp.minimum(x, 0.0) - jnp.log1p(jnp.exp(-jnp.abs(x)))


def _iota(shape, dim):
    return lax.broadcasted_iota(jnp.int32, shape, dim)


def _const_spec(a):
    nd = a.ndim
    return pl.BlockSpec(a.shape, lambda *_: (0,) * nd)


def _params(sem):
    return pltpu.CompilerParams(dimension_semantics=sem, vmem_limit_bytes=VMEM_LIMIT)


def _proj_kernel(x_ref, g_ref, wlp_ref, whp_ref, wms_ref, ra_ref, rb_ref, rc_ref,
                 qk_ref, v_ref, o_ref, zm_ref, q_ref, qr_ref, kvc_ref, kvs_ref, kvw_ref, zn_ref, ms_ref):
    x = x_ref[...]
    xn = x * lax.rsqrt(jnp.mean(x * x, axis=-1, keepdims=True) + NORM_EPS) * g_ref[...]
    xb = xn.astype(BF16)
    ra, rb, rc = ra_ref[...], rb_ref[...], rc_ref[...]

    def rope(a):
        return a * ra + pltpu.roll(a, LANES - ROT_HALF, 1) * rb + pltpu.roll(a, ROT_HALF, 1) * rc

    def lp(lo, hi):
        return jnp.dot(xb, wlp_ref[:, lo:hi], preferred_element_type=F32)

    qk_ref[...] = lp(0, 1024)
    v_ref[...] = lp(1024, 1536)
    o_ref[...] = lp(1536, 2048)
    zm_ref[...] = lp(2048, 2560)
    kvs = lp(2560, 2816)
    kvs_ref[:, :LANES] = rope(kvs[:, :LANES])
    kvs_ref[:, LANES:] = kvs[:, LANES:]
    kvw = lp(2816, 3072)
    kvw_ref[:, :LANES] = rope(kvw[:, :LANES])
    kvw_ref[:, LANES:] = kvw[:, LANES:]
    zn_ref[...] = lp(3072, 3584)
    hp = jnp.dot(xn, whp_ref[...], preferred_element_type=F32, precision=HI)
    q_ref[...] = hp[:, :D_N]
    for c in range(D_N // LANES):
        qr_ref[:, c * LANES:(c + 1) * LANES] = rope(hp[:, c * LANES:(c + 1) * LANES])
    kvc_ref[...] = hp[:, D_N:]
    ms_ref[...] = jnp.dot(xn, wms_ref[...], preferred_element_type=F32, precision=HI)


def _rope_tables(pos):
    inv = ROPE_THETA ** (-jnp.arange(ROT_HALF, dtype=F32) / ROT_HALF)
    ang = pos.astype(F32)[:, None] * inv[None, :]
    cos, sin = jnp.cos(ang), jnp.sin(ang)
    one = jnp.ones((pos.shape[0], N_HEAD_DIM - 2 * ROT_HALF), F32)
    zero8 = jnp.zeros_like(cos)
    zero = jnp.zeros_like(one)
    ra = jnp.concatenate([cos, cos, one], axis=1)
    rb = jnp.concatenate([-sin, zero8, zero], axis=1)
    rc = jnp.concatenate([zero8, sin, zero], axis=1)
    return tuple(jnp.tile(t, (1, LANES // N_HEAD_DIM)) for t in (ra, rb, rc))


def _project(x2d, g_pre, w_lp, w_hp, w_ms, tabs, tab_blocks):
    n = x2d.shape[0]
    r = PROJ_ROWS
    row = lambda w: pl.BlockSpec((r, w), lambda i: (i, 0))
    tab = pl.BlockSpec((r, LANES), lambda i: (i % tab_blocks, 0))
    widths = (1024, 512, 512, 512, 512, 512, 256, 256, 256, 512, 128)
    return pl.pallas_call(
        _proj_kernel,
        grid=(n // r,),
        in_specs=[row(D_MODEL), _const_spec(g_pre), _const_spec(w_lp), _const_spec(w_hp), _const_spec(w_ms), tab, tab, tab],
        out_specs=[row(w) for w in widths],
        out_shape=[jax.ShapeDtypeStruct((n, w), F32) for w in widths],
        compiler_params=_params(("parallel",)),
        name="project",
    )(x2d, g_pre, w_lp, w_hp, w_ms, *tabs)


def _mlstm_kernel(qk_ref, v_ref, ms_ref, mst_ref, tail_ref, c0_ref, n0_ref, m0_ref, wc_ref, bc_ref, gbr_ref, gbc_ref,
                  h_ref, c_ref, n_ref, m_ref, xp_sc, c_sc, n_sc, m_sc, *, chunk, n_valid):
    L = chunk
    ci = pl.program_id(1)

    @pl.when(ci == 0)
    def _():
        xp_sc[0:SUBLANES, :] = tail_ref[0]
        c_sc[...] = c0_ref[0]
        n_sc[0:M_HEADS, :] = n0_ref[0]
        m_sc[0:M_HEADS, :] = m0_ref[0]

    xp_sc[SUBLANES:SUBLANES + L, :] = qk_ref[0]
    y = bc_ref[...]
    for j in range(CONV_W):
        y = y + xp_sc[pl.ds(SUBLANES - (CONV_W - 1) + j, L), :] * wc_ref[j:j + 1, :]
    qk = y * jax.nn.sigmoid(y)
    xp_sc[0:SUBLANES, :] = xp_sc[L:L + SUBLANES, :]

    row = _iota((L, L), 0)
    col = _iota((L, L), 1)
    tri = row >= col
    g = ms_ref[0] + gbr_ref[...]
    fl = _log_sigmoid(g)
    g_t = mst_ref[0] + gbc_ref[...]
    fl_t = _log_sigmoid(g_t)
    if n_valid < L:
        ok = _iota((L, LANES), 0) < n_valid
        g = jnp.where(ok, g, NEG)
        fl = jnp.where(ok, fl, 0.0)
        ok_t = _iota((SUBLANES, L), 1) < n_valid
        g_t = jnp.where(ok_t, g_t, NEG)
        fl_t = jnp.where(ok_t, fl_t, 0.0)
    cum = _dot(jnp.where(tri, 1.0, 0.0), fl, HI)
    cum_t = _dot(fl_t, jnp.where(row <= col, 1.0, 0.0), HI)

    for h in range(M_HEADS):
        hs = slice(h * M_HEAD_DIM, (h + 1) * M_HEAD_DIM)
        q = qk[:, hs]
        k = qk[:, D_M + h * M_HEAD_DIM:D_M + (h + 1) * M_HEAD_DIM] * (M_HEAD_DIM ** -0.5)
        v = v_ref[0, :, hs]
        ic = g[:, h:h + 1]
        bcol = cum[:, M_HEADS + h:M_HEADS + h + 1]
        ir = g_t[h:h + 1, :]
        brow = cum_t[M_HEADS + h:M_HEADS + h + 1, :]
        m_prev = m_sc[h:h + 1, 0:1]
        c_old = c_sc[h]
        n_old = n_sc[h:h + 1, :]
        log_d = jnp.where(tri, bcol - brow + ir, NEG)
        m_inter = bcol + m_prev
        m_new = jnp.maximum(m_inter, jnp.max(log_d, axis=1, keepdims=True))
        w = _dot_nt(q, k, HI) * jnp.exp(log_d - m_new)
        inter = jnp.exp(m_inter - m_new)
        num = _dot(w, v, HI) + inter * _dot_nt(q, c_old, HI)
        nq = jnp.sum(w, axis=1, keepdims=True) + inter * jnp.sum(q * n_old, axis=1, keepdims=True)
        h_ref[0, :, hs] = num / jnp.maximum(jnp.abs(nq), jnp.exp(-m_new))
        m_end = m_new[L - 1:L, :]
        b_end = bcol[L - 1:L, :]
        decay = jnp.exp(b_end + m_prev - m_end)
        wk = jnp.exp(b_end - bcol + ic - m_end)
        c_sc[h] = decay * c_old + _dot_tn(wk * v, k, HI)
        n_sc[h:h + 1, :] = decay * n_old + jnp.sum(wk * k, axis=0, keepdims=True)
        m_sc[h:h + 1, :] = jnp.broadcast_to(m_end, (1, LANES))

    @pl.when(ci == pl.num_programs(1) - 1)
    def _():
        c_ref[0] = c_sc[...]
        n_ref[0] = n_sc[0:M_HEADS, :]
        m_ref[0] = m_sc[0:M_HEADS, :]


def _mlstm(qk, v, ms, tail, c0, n0, m0, w_conv, b_conv, b_igate, b_fgate, chunk, n_valid):
    b, t, _ = qk.shape
    nc = t // chunk
    gb = jnp.concatenate([b_igate, b_fgate])
    gb_row = jnp.zeros((1, LANES), F32).at[0, :2 * M_HEADS].set(gb)
    gb_col = gb.reshape(2 * M_HEADS, 1)
    ms_t = jnp.swapaxes(ms[:, :, :2 * M_HEADS], 1, 2)
    seq = lambda w: pl.BlockSpec((1, chunk, w), lambda i, c: (i, c, 0))
    per_b = lambda a: pl.BlockSpec((1,) + a.shape[1:], lambda i, c: (i,) + (0,) * (a.ndim - 1))
    kern = functools.partial(_mlstm_kernel, chunk=chunk, n_valid=n_valid)
    out_shape = [jax.ShapeDtypeStruct((b, t, D_M), F32),
                 jax.ShapeDtypeStruct((b, M_HEADS, M_HEAD_DIM, M_HEAD_DIM), F32),
                 jax.ShapeDtypeStruct((b, M_HEADS, M_HEAD_DIM), F32),
                 jax.ShapeDtypeStruct((b, M_HEADS, LANES), F32)]
    return pl.pallas_call(
        kern,
        grid=(b, nc),
        in_specs=[seq(2 * D_M), seq(D_M), seq(LANES),
                  pl.BlockSpec((1, 2 * M_HEADS, chunk), lambda i, c: (i, 0, c)),
                  per_b(tail), per_b(c0), per_b(n0), per_b(m0),
                  _const_spec(w_conv), _const_spec(b_conv.reshape(1, -1)), _const_spec(gb_row), _const_spec(gb_col)],
        out_specs=[seq(D_M), per_b(out_shape[1]), per_b(out_shape[2]), per_b(out_shape[3])],
        out_shape=out_shape,
        scratch_shapes=[pltpu.VMEM((chunk + SUBLANES, 2 * D_M), F32),
                        pltpu.VMEM((M_HEADS, M_HEAD_DIM, M_HEAD_DIM), F32),
                        pltpu.VMEM((SUBLANES, M_HEAD_DIM), F32),
                        pltpu.VMEM((SUBLANES, LANES), F32)],
        compiler_params=_params(("parallel", "arbitrary")),
        name="mlstm",
    )(qk, v, ms, ms_t, tail, c0, n0, m0, w_conv, b_conv.reshape(1, -1), gb_row, gb_col)


def _sub_block_projections(xk_of_p, xv_of_p, wk_hi_ref, wk_lo_ref, wv_ref):
    acc_k = acc_v = None
    for pp in range(D_CMP // 2):
        rs = slice(pp * 2 * LANES, (pp + 1) * 2 * LANES)
        xk = jnp.concatenate([xk_of_p(2 * pp), xk_of_p(2 * pp + 1)], axis=1)
        xv = jnp.concatenate([xv_of_p(2 * pp), xv_of_p(2 * pp + 1)], axis=1)
        tk = _dot_3pass(xk, wk_hi_ref[rs, :], wk_lo_ref[rs, :])
        tv = _dot(xv.astype(BF16), wv_ref[rs, :])
        acc_k = tk if acc_k is None else acc_k + tk
        acc_v = tv if acc_v is None else acc_v + tv
    return jnp.concatenate([acc_k, acc_v], axis=1)


def _compress_bias(pek_ref, pev_ref, wk_hi_ref, wk_lo_ref, wv_ref):
    b8 = _sub_block_projections(lambda p: pek_ref[:, p * LANES:(p + 1) * LANES],
                                lambda p: pev_ref[:, p * LANES:(p + 1) * LANES], wk_hi_ref, wk_lo_ref, wv_ref)
    return (b8[0:1, 0:LANES] + b8[1:2, LANES:2 * LANES],
            b8[0:1, 2 * LANES:3 * LANES] + b8[1:2, 3 * LANES:4 * LANES])


def _combine_sub_blocks(fs_sc, n_sub, bias_k, bias_v):
    k = fs_sc[0:n_sub, 0:LANES] + fs_sc[pl.ds(1, n_sub), LANES:2 * LANES] + bias_k
    v = fs_sc[0:n_sub, 2 * LANES:3 * LANES] + fs_sc[pl.ds(1, n_sub), 3 * LANES:4 * LANES] + bias_v
    return k, v


def _compress_kernel(x_ref, wk_hi_ref, wk_lo_ref, wv_ref, pek_ref, pev_ref, o_ref, fs_sc):
    n_sub = x_ref.shape[1]
    w = (wk_hi_ref, wk_lo_ref, wv_ref)
    fs_sc[0:n_sub, :] = _sub_block_projections(
        lambda p: x_ref[0, :, p * KV_W:p * KV_W + LANES],
        lambda p: x_ref[0, :, p * KV_W + LANES:(p + 1) * KV_W], *w)
    fs_sc[n_sub:n_sub + SUBLANES, :] = jnp.zeros((SUBLANES, 2 * KV_W), F32)
    k, v = _combine_sub_blocks(fs_sc, n_sub, *_compress_bias(pek_ref, pev_ref, *w))
    o_ref[0, :, 0:LANES] = k
    o_ref[0, :, LANES:] = v


def _compress(x, cw):
    b, n_sub, _ = x.shape
    return pl.pallas_call(
        _compress_kernel,
        grid=(b,),
        in_specs=[pl.BlockSpec((1, n_sub, CMP_COLS), lambda i: (i, 0, 0))] + [_const_spec(a) for a in cw],
        out_specs=pl.BlockSpec((1, n_sub, KV_W), lambda i: (i, 0, 0)),
        out_shape=jax.ShapeDtypeStruct((b, n_sub, KV_W), F32),
        scratch_shapes=[pltpu.VMEM((n_sub + SUBLANES, 2 * KV_W), F32)],
        compiler_params=_params(("parallel",)),
        name="compress",
    )(x, *cw)


def _sub_proj_kernel(x_ref, wk_hi_ref, wk_lo_ref, wv_ref, o_ref):
    o_ref[...] = _sub_block_projections(
        lambda p: x_ref[:, p * KV_W:p * KV_W + LANES],
        lambda p: x_ref[:, p * KV_W + LANES:(p + 1) * KV_W], wk_hi_ref, wk_lo_ref, wv_ref)


def _sub_proj(x, cw):
    return pl.pallas_call(
        _sub_proj_kernel,
        out_shape=jax.ShapeDtypeStruct((x.shape[0], 2 * KV_W), F32),
        compiler_params=pltpu.CompilerParams(vmem_limit_bytes=VMEM_LIMIT),
        name="sub_proj_new",
    )(x, *cw[:3])


def _compress_weights(w_cmp, pe_cmp):
    w5 = w_cmp.reshape(2, 2, D_CMP, N_HEAD_DIM, N_HEAD_DIM)
    eye = jnp.eye(KV_HEADS, dtype=F32)
    stack = lambda wc: jnp.einsum('fpde,hH->phdfHe', wc, eye).reshape(D_CMP * LANES, 2 * LANES)
    wk_hi, wk_lo = _split_bf16(stack(w5[0]))
    wv = stack(w5[1]).astype(BF16)
    pe4 = pe_cmp.reshape(2, 2, D_CMP, N_HEAD_DIM)
    rows = lambda pc: jnp.zeros((SUBLANES, D_CMP * LANES), F32).at[0:2].set(
        jnp.broadcast_to(pc[:, :, None, :], (2, D_CMP, KV_HEADS, N_HEAD_DIM)).reshape(2, D_CMP * LANES))
    return wk_hi, wk_lo, wv, rows(pe4[0]), rows(pe4[1])


def _importance_matrix(n_sel_pad, n_cmp_pad):
    j = jnp.arange(n_sel_pad)[:, None]
    n = jnp.arange(n_cmp_pad)[None, :]
    r = n - SEL_RATIO * j + 1
    w = jnp.asarray(SLC_W, F32)
    return jnp.where((r >= 0) & (r < len(SLC_W)), w[jnp.clip(r, 0, len(SLC_W) - 1)], 0.0).astype(F32)


def _importance(mt_bf16, p_sum):
    p_hi, p_lo = _split_bf16(p_sum)
    return _dot(mt_bf16, p_hi) + _dot(mt_bf16, p_lo)


def _rank_select(v, n_cand):
    n_rows = v.shape[0]
    slabs = [v[r:r + SUBLANES, :] for r in range(0, n_rows, SUBLANES)]
    cnts = [jnp.zeros(s.shape, F32) for s in slabs]
    j_in = _iota(slabs[0].shape, 0)
    for jp in range(n_cand):
        rowv = slabs[jp // SUBLANES][jp % SUBLANES:jp % SUBLANES + 1, :]
        for si, s in enumerate(slabs):
            if si * SUBLANES > jp:
                beats = rowv >= s
            elif si * SUBLANES + SUBLANES - 1 < jp:
                beats = rowv > s
            else:
                beats = (rowv > s) | ((rowv == s) & (j_in > jp - si * SUBLANES))
            cnts[si] = cnts[si] + jnp.where(beats, 1.0, 0.0)
    return jnp.concatenate([jnp.where(c < TOP_N, 1.0, 0.0) for c in cnts], axis=0)


def _selection_bonus(j, t):
    cur = t // L_SEL
    forced = (j == 0) | (j == cur) | (j == cur - 1)
    return jnp.where(j > cur, NEG, jnp.where(forced, FORCE, 0.0))


def _nsa_prompt_kernel(q_ref, qr_ref, g_ref, kc_ref, vct_ref, mt_ref, ks_ref, vs_ref, kw_ref, vw_ref, e_ref,
                       o_ref, m_sc, l_sc, acc_sc):
    qb = pl.program_id(2)
    start = qb * Q_BLOCK
    rows = GQA * Q_BLOCK
    n_cmp = kc_ref.shape[2]
    n_sel = mt_ref.shape[0]

    q = q_ref[0, 0, 0] * QK_SCALE
    s_t = _dot_nt_3pass(kc_ref[0, 0], q)
    n_i = _iota((n_cmp, rows), 0)
    t_l = start + (_iota((n_cmp, rows), 1) & (Q_BLOCK - 1))
    ok = (D_CMP * n_i + (L_CMP - 1)) <= t_l
    s_t = jnp.where(ok, s_t, NEG)
    p_t = jnp.where(ok, jnp.exp(s_t - jnp.max(s_t, axis=0, keepdims=True)), 0.0)
    l_t = jnp.sum(p_t, axis=0, keepdims=True)
    p_t = p_t / jnp.where(l_t > 0.0, l_t, 1.0)
    o_c = _dot(vct_ref[0, 0].astype(BF16), p_t.astype(BF16)).T

    p_sum = p_t[:, 0:Q_BLOCK]
    for gi in range(1, GQA):
        p_sum = p_sum + p_t[:, gi * Q_BLOCK:(gi + 1) * Q_BLOCK]
    imp_t = _importance(mt_ref[...], p_sum)
    j = _iota((n_sel, Q_BLOCK), 0)
    t_q = start + _iota((n_sel, Q_BLOCK), 1)
    sel = _rank_select(imp_t + _selection_bonus(j, t_q), n_sel).T.astype(BF16)

    qr = (qr_ref[0, 0, 0] * QK_SCALE).astype(BF16)
    m_sc[...] = jnp.full(m_sc.shape, NEG, F32)
    l_sc[...] = jnp.zeros(l_sc.shape, F32)
    acc_sc[...] = jnp.zeros(acc_sc.shape, F32)
    t_r = start + (_iota((rows, KEY_TILE), 0) & (Q_BLOCK - 1))
    lane_k = _iota((rows, KEY_TILE), 1)

    def tile(kt, carry):
        k0 = pl.multiple_of(kt * KEY_TILE, KEY_TILE)
        s = _dot_nt(qr, ks_ref[0, 0, pl.ds(k0, KEY_TILE), :])
        picked = _dot(sel, e_ref[:, pl.ds(k0, KEY_TILE)])
        picked = jnp.concatenate([picked] * GQA, axis=0)
        ok_s = (picked > 0.5) & ((k0 + lane_k) <= t_r)
        s = jnp.where(ok_s, s, NEG)
        m_old = m_sc[...]
        m_new = jnp.maximum(m_old, jnp.max(s, axis=1, keepdims=True))
        p = jnp.where(ok_s, jnp.exp(s - m_new), 0.0)
        alpha = jnp.exp(m_old - m_new)
        l_sc[...] = alpha * l_sc[...] + jnp.sum(p, axis=1, keepdims=True)
        acc_sc[...] = alpha * acc_sc[...] + _dot(p.astype(BF16), vs_ref[0, 0, pl.ds(k0, KEY_TILE), :])
        m_sc[...] = m_new
        return carry

    lax.fori_loop(0, (start + Q_BLOCK + KEY_TILE - 1) // KEY_TILE, tile, 0)
    o_s = acc_sc[...] / l_sc[...]

    span = WINDOW + Q_BLOCK
    w0 = pl.multiple_of(jnp.maximum(start - WINDOW, 0), Q_BLOCK)
    s = _dot_nt(qr, kw_ref[0, 0, pl.ds(w0, span), :])
    dt = (start + (_iota((rows, span), 0) & (Q_BLOCK - 1))) - (w0 + _iota((rows, span), 1))
    ok_w = (dt >= 0) & (dt < WINDOW)
    s = jnp.where(ok_w, s, NEG)
    p = jnp.where(ok_w, jnp.exp(s - jnp.max(s, axis=1, keepdims=True)), 0.0)
    o_w = _dot(p.astype(BF16), vw_ref[0, 0, pl.ds(w0, span), :]) / jnp.sum(p, axis=1, keepdims=True)

    gate = jax.nn.sigmoid(g_ref[0, 0, 0])
    o_ref[0, 0, 0] = gate[:, 0:1] * o_c + gate[:, 1:2] * o_s + gate[:, 2:3] * o_w


def _nsa_prompt(q_r, qrot_r, g_r, k_c, v_ct, m_t, k_s, v_s, k_w, v_w, expand):
    b, _, nqb, rows, _ = q_r.shape
    blk = lambda w: pl.BlockSpec((1, 1, 1, rows, w), lambda i, h, c: (i, h, c, 0, 0))
    per_bh = lambda a: pl.BlockSpec((1, 1) + a.shape[2:], lambda i, h, c: (i, h, 0, 0))
    return pl.pallas_call(
        _nsa_prompt_kernel,
        grid=(b, KV_HEADS, nqb),
        in_specs=[blk(N_HEAD_DIM), blk(N_HEAD_DIM), blk(3), per_bh(k_c), per_bh(v_ct), _const_spec(m_t),
                  per_bh(k_s), per_bh(v_s), per_bh(k_w), per_bh(v_w), _const_spec(expand)],
        out_specs=blk(N_HEAD_DIM),
        out_shape=jax.ShapeDtypeStruct((b, KV_HEADS, nqb, rows, N_HEAD_DIM), F32),
        scratch_shapes=[pltpu.VMEM((rows, 1), F32), pltpu.VMEM((rows, 1), F32), pltpu.VMEM((rows, N_HEAD_DIM), F32)],
        compiler_params=_params(("parallel", "parallel", "arbitrary")),
        name="nsa_prompt",
    )(q_r, qrot_r, g_r, k_c, v_ct, m_t, k_s, v_s, k_w, v_w, expand)


def _gather_pages(pt_ref, cache_hbm, buf, sem, n_pages):
    b = pl.program_id(0)
    slot = b % 2

    def page_copy(seq, j, s):
        return pltpu.make_async_copy(cache_hbm.at[pt_ref[seq, j]], buf.at[s, j], sem.at[s])

    def issue(seq, s):
        def body(j, c):
            page_copy(seq, j, s).start()
            return c
        lax.fori_loop(0, n_pages, body, 0)

    @pl.when(b == 0)
    def _():
        issue(0, 0)

    @pl.when(b + 1 < pl.num_programs(0))
    def _():
        issue(b + 1, 1 - slot)

    def wait_body(j, c):
        page_copy(b, j, slot).wait()
        return c
    lax.fori_loop(0, n_pages, wait_body, 0)
    return slot


def _nsa_sample_cmp_kernel(pt_ref, cache_hbm, wk_hi_ref, wk_lo_ref, wv_ref, pek_ref, pev_ref, fsn_ref, q_ref,
                           mt_ref, gq_ref, oc_ref, sel_ref, xbuf, xk_sc, xv_sc, fs_sc, sem, *, n_pages, past, t_new):
    sub_per_page = PAGE_SIZE // D_CMP
    n_sub = n_pages * sub_per_page
    half = KV_W // 2
    slot = _gather_pages(pt_ref, cache_hbm, xbuf, sem, n_pages)

    def to_rows(j, c):
        r0 = pl.multiple_of(j * PAGE_SIZE, PAGE_SIZE)
        xk_sc[pl.ds(r0, PAGE_SIZE), :] = xbuf[slot, j, 0:half, :].T
        xv_sc[pl.ds(r0, PAGE_SIZE), :] = xbuf[slot, j, half:KV_W, :].T
        return c
    lax.fori_loop(0, n_pages, to_rows, 0)

    w = (wk_hi_ref, wk_lo_ref, wv_ref)
    fs_sc[0:n_sub, :] = _sub_block_projections(lambda p: xk_sc[pl.ds(p, n_sub, stride=D_CMP), :],
                                               lambda p: xv_sc[pl.ds(p, n_sub, stride=D_CMP), :], *w)
    fs_sc[n_sub:n_sub + SUBLANES, :] = fsn_ref[0]
    kc, vc = _combine_sub_blocks(fs_sc, n_sub, *_compress_bias(pek_ref, pev_ref, *w))

    n_sel = mt_ref.shape[0]
    rows = GQA * t_new
    for h in range(KV_HEADS):
        k_h = kc[:, h * N_HEAD_DIM:(h + 1) * N_HEAD_DIM]
        v_h = vc[:, h * N_HEAD_DIM:(h + 1) * N_HEAD_DIM]
        q_h = q_ref[0, h * rows:(h + 1) * rows, :] * QK_SCALE
        s_t = _dot_nt_3pass(k_h, q_h)
        t_l = past + lax.rem(_iota((n_sub, rows), 1), t_new)
        ok = (D_CMP * _iota((n_sub, rows), 0) + (L_CMP - 1)) <= t_l
        s_t = jnp.where(ok, s_t, NEG)
        p_t = jnp.where(ok, jnp.exp(s_t - jnp.max(s_t, axis=0, keepdims=True)), 0.0)
        l_t = jnp.sum(p_t, axis=0, keepdims=True)
        p_t = p_t / jnp.where(l_t > 0.0, l_t, 1.0)
        imp_t = _importance(mt_ref[...], _dot(p_t, gq_ref[...], HI))
        j = _iota((n_sel, t_new), 0)
        t_q = past + _iota((n_sel, t_new), 1)
        sel_ref[0, h] = _rank_select(imp_t + _selection_bonus(j, t_q), n_sel)
        oc_ref[0, :, h * rows:(h + 1) * rows] = _dot_tn(v_h.astype(BF16), p_t.astype(BF16))


def _nsa_sample_cmp(page_table, cache_t, cw, fs_new, q_rows, m_t, gq, past, t_new):
    b, n_pages = page_table.shape
    n_sub = n_pages * (PAGE_SIZE // D_CMP)
    rows = q_rows.shape[1]
    n_sel_pad = m_t.shape[0]
    kern = functools.partial(_nsa_sample_cmp_kernel, n_pages=n_pages, past=past, t_new=t_new)
    const = lambda a: pl.BlockSpec(a.shape, lambda i, pt: (0,) * a.ndim)
    per_b = lambda a: pl.BlockSpec((1,) + a.shape[1:], lambda i, pt: (i,) + (0,) * (a.ndim - 1))
    out_shape = [jax.ShapeDtypeStruct((b, N_HEAD_DIM, rows), F32),
                 jax.ShapeDtypeStruct((b, KV_HEADS, n_sel_pad, t_new), F32)]
    return pl.pallas_call(
        kern,
        grid_spec=pltpu.PrefetchScalarGridSpec(
            num_scalar_prefetch=1,
            grid=(b,),
            in_specs=[pl.BlockSpec(memory_space=pl.ANY)] + [const(a) for a in cw]
                     + [per_b(fs_new), per_b(q_rows), const(m_t), const(gq)],
            out_specs=[per_b(out_shape[0]), per_b(out_shape[1])],
            scratch_shapes=[pltpu.VMEM((2, n_pages, KV_W, PAGE_SIZE), F32),
                            pltpu.VMEM((n_pages * PAGE_SIZE, KV_W // 2), F32),
                            pltpu.VMEM((n_pages * PAGE_SIZE, KV_W // 2), F32),
                            pltpu.VMEM((n_sub + SUBLANES, 2 * KV_W), F32),
                            pltpu.SemaphoreType.DMA((2,))]),
        out_shape=out_shape,
        compiler_params=_params(("arbitrary",)),
        name="nsa_sample_cmp",
    )(page_table, cache_t, *cw, fs_new, q_rows, m_t, gq)


def _nsa_sample_attn_kernel(pt_ref, cache_hbm, qbd_ref, selt_ref, win_ref, kvsn_ref, kvwn_ref, oc_ref, g_ref,
                            o_ref, wout_ref, kbuf, s_sc, sem, *, n_pages, past, t_new):
    slot = _gather_pages(pt_ref, cache_hbm, kbuf, sem, n_pages)
    cols = qbd_ref.shape[2]
    half = KV_W // 2
    qbd = (qbd_ref[0] * QK_SCALE).astype(BF16)
    n_buf = win_ref.shape[2]
    pages_per_tile = KEY_TILE // PAGE_SIZE
    blocks_per_tile = KEY_TILE // L_SEL
    n_tiles = n_pages // pages_per_tile
    n_pb = past // L_SEL

    def t_of(shape):
        return past + lax.rem(_iota(shape, 1), t_new)

    def tile_rows(c, lo):
        return jnp.concatenate([kbuf[slot, c * pages_per_tile + i, lo:lo + half, :] for i in range(pages_per_tile)],
                               axis=1)

    def new_rows(ref, extra_ok):
        kn = ref[0]
        s = _dot(kn[:, :half].astype(BF16), qbd)
        ki = _iota((SUBLANES, cols), 0)
        ok = (ki < t_new) & ((past + ki) <= t_of((SUBLANES, cols))) & extra_ok
        return jnp.where(ok, s, NEG), ok, kn[:, half:].astype(BF16)

    def finish(acc, l):
        own = jnp.where(_iota((N_HEAD_DIM, cols), 1) < cols // KV_HEADS, acc[:N_HEAD_DIM, :], acc[N_HEAD_DIM:, :])
        return own / l

    def score_tile(c, mx):
        k0 = pl.multiple_of(c * KEY_TILE, KEY_TILE)
        s = _dot(tile_rows(c, 0).T.astype(BF16), qbd)
        picked = jnp.concatenate(
            [jnp.broadcast_to(selt_ref[0, pl.ds(c * blocks_per_tile + i, 1), :], (L_SEL, cols))
             for i in range(blocks_per_tile)], axis=0)
        s = jnp.where(picked > 0.5, s, NEG)
        s_sc[pl.ds(k0, KEY_TILE), :] = s
        return jnp.maximum(mx, jnp.max(s, axis=0, keepdims=True))

    mx = lax.fori_loop(0, n_tiles, score_tile, jnp.full((1, cols), NEG, F32))
    s_n, ok_n, v_n = new_rows(kvsn_ref, selt_ref[0, n_pb:n_pb + 1, :] > 0.5)
    mx = jnp.maximum(mx, jnp.max(s_n, axis=0, keepdims=True))
    p_n = jnp.where(ok_n, jnp.exp(s_n - mx), 0.0)

    def value_tile(c, carry):
        l, acc = carry
        k0 = pl.multiple_of(c * KEY_TILE, KEY_TILE)
        s = s_sc[pl.ds(k0, KEY_TILE), :]
        p = jnp.where(s > 0.5 * NEG, jnp.exp(s - mx), 0.0)
        acc = acc + _dot(tile_rows(c, half).astype(BF16), p.astype(BF16))
        return l + jnp.sum(p, axis=0, keepdims=True), acc

    l, acc = lax.fori_loop(0, n_tiles, value_tile,
                           (jnp.sum(p_n, axis=0, keepdims=True), _dot_tn(v_n, p_n.astype(BF16))))
    o_s = finish(acc, l)

    win = win_ref[0]
    s = _dot(win[:half, :].T.astype(BF16), qbd)
    dt = t_of((n_buf, cols)) - (past - n_buf + _iota((n_buf, cols), 0))
    ok_w = (dt >= 0) & (dt < WINDOW)
    s = jnp.where(ok_w, s, NEG)
    s_n, ok_n, v_n = new_rows(kvwn_ref, True)
    mx = jnp.maximum(jnp.max(s, axis=0, keepdims=True), jnp.max(s_n, axis=0, keepdims=True))
    p = jnp.where(ok_w, jnp.exp(s - mx), 0.0)
    p_n = jnp.where(ok_n, jnp.exp(s_n - mx), 0.0)
    acc = _dot(win[half:, :].astype(BF16), p.astype(BF16)) + _dot_tn(v_n, p_n.astype(BF16))
    o_w = finish(acc, jnp.sum(p, axis=0, keepdims=True) + jnp.sum(p_n, axis=0, keepdims=True))

    gate = jax.nn.sigmoid(g_ref[0])
    o_ref[0] = gate[0:1, :] * oc_ref[0] + gate[1:2, :] * o_s + gate[2:3, :] * o_w

    rolled = pltpu.roll(win, n_buf - t_new, 1)
    place = jnp.where((_iota((SUBLANES, LANES), 1) == (LANES - t_new) + _iota((SUBLANES, LANES), 0))
                      & (_iota((SUBLANES, LANES), 0) < t_new), 1.0, 0.0)
    tail = _dot_tn(kvwn_ref[0], place, HI)
    wout_ref[0, :, 0:n_buf - LANES] = rolled[:, 0:n_buf - LANES]
    wout_ref[0, :, n_buf - LANES:] = jnp.where(_iota((KV_W, LANES), 1) >= LANES - t_new, tail,
                                               rolled[:, n_buf - LANES:])


def _nsa_sample_attn(page_table, cache_t, qbd_t, sel_t, win_t, kvs_new, kvw_new, o_c, g_t, past, t_new):
    b, n_pages = page_table.shape
    cols = qbd_t.shape[2]
    kern = functools.partial(_nsa_sample_attn_kernel, n_pages=n_pages, past=past, t_new=t_new)
    per_b = lambda a: pl.BlockSpec((1,) + a.shape[1:], lambda i, pt: (i,) + (0,) * (a.ndim - 1))
    out_shape = [jax.ShapeDtypeStruct(o_c.shape, F32), jax.ShapeDtypeStruct(win_t.shape, F32)]
    return pl.pallas_call(
        kern,
        grid_spec=pltpu.PrefetchScalarGridSpec(
            num_scalar_prefetch=1,
            grid=(b,),
            in_specs=[pl.BlockSpec(memory_space=pl.ANY), per_b(qbd_t), per_b(sel_t), per_b(win_t), per_b(kvs_new),
                      per_b(kvw_new), per_b(o_c), per_b(g_t)],
            out_specs=[per_b(o_c), per_b(win_t)],
            scratch_shapes=[pltpu.VMEM((2, n_pages, KV_W, PAGE_SIZE), F32),
                            pltpu.VMEM((past, cols), F32),
                            pltpu.SemaphoreType.DMA((2,))]),
        out_shape=out_shape,
        compiler_params=_params(("arbitrary",)),
        name="nsa_sample_attn",
    )(page_table, cache_t, qbd_t, sel_t, win_t, kvs_new, kvw_new, o_c, g_t)


def _merge_kernel(x_ref, h_ref, om_ref, zm_ref, on_ref, zn_ref, w_ref, g_ref, y_ref):
    zm = zm_ref[...]
    zn = zn_ref[...]
    a = h_ref[...] * jax.nn.sigmoid(om_ref[...]) * (zm * jax.nn.sigmoid(zm))
    c = on_ref[...] * (zn * jax.nn.sigmoid(zn))
    y = _dot(a.astype(BF16), w_ref[0:D_M, :]) + _dot(c.astype(BF16), w_ref[D_M:, :])
    yn = y * lax.rsqrt(jnp.mean(y * y, axis=-1, keepdims=True) + NORM_EPS)
    y_ref[...] = x_ref[...] + yn * g_ref[...]


def _merge(x2d, h_m, o_m, z_m, o_n, z_n, w_out, g_post):
    n = x2d.shape[0]
    r = PROJ_ROWS
    row = lambda w: pl.BlockSpec((r, w), lambda i: (i, 0))
    return pl.pallas_call(
        _merge_kernel,
        grid=(n // r,),
        in_specs=[row(D_MODEL), row(D_M), row(D_M), row(D_M), row(D_N), row(D_N), _const_spec(w_out), _const_spec(g_post)],
        out_specs=row(D_MODEL),
        out_shape=jax.ShapeDtypeStruct((n, D_MODEL), F32),
        compiler_params=_params(("parallel",)),
        name="merge",
    )(x2d, h_m, o_m, z_m, o_n, z_n, w_out, g_post)


def _split_heads(kv, c):
    b, t, _ = kv.shape
    return kv.reshape(b, t, 2, KV_HEADS, N_HEAD_DIM)[:, :, c].transpose(0, 2, 1, 3)


def _group_rows(a, width):
    b, t, _ = a.shape
    a = a.reshape(b, t // Q_BLOCK, Q_BLOCK, KV_HEADS, GQA, width)
    return a.transpose(0, 3, 1, 4, 2, 5).reshape(b, KV_HEADS, t // Q_BLOCK, GQA * Q_BLOCK, width)


def _ungroup_rows(o, b, t):
    o = o.reshape(b, KV_HEADS, t // Q_BLOCK, GQA, Q_BLOCK, N_HEAD_DIM)
    return o.transpose(0, 2, 4, 1, 3, 5).reshape(b, t, D_N)


def _sample_rows(a, width):
    b, t, _ = a.shape
    return a.reshape(b, t, KV_HEADS * GQA, width).transpose(0, 2, 1, 3).reshape(b, KV_HEADS * GQA * t, width)


def _pages_as_stored(cache):
    n, rows = cache.shape[:2]
    return cache.reshape(n, rows, KV_W).transpose(0, 2, 1)


def kernel(x_prompt, x_sample, cache_kv_cmp, cache_kv_sel, cache_kv_win, state_conv, state_C, state_n, state_m,
           page_table, g_pre, w_in, b_igate, b_fgate, w_conv, b_conv, w_cmp, pe_cmp, w_out, g_post):
    assert g_pre.shape[0] == 1, "single-layer trunk"
    bp, t_p, _ = x_prompt.shape
    bs, t_s, _ = x_sample.shape
    n_pages = page_table.shape[1]
    past = n_pages * PAGE_SIZE
    assert cache_kv_win.shape[2] == WINDOW and past >= WINDOW
    w = w_in[0]

    w_lp = jnp.concatenate([w[:, 0:2560], w[:, 3336:3848], w[:, 3872:4384]], axis=1).astype(BF16)
    w_hp = w[:, 2568:3336]
    w_ms = jnp.concatenate([w[:, 2560:2568], w[:, 3848:3872], jnp.zeros((D_MODEL, LANES - 32), F32)], axis=1)
    g_pre2 = g_pre[0].reshape(1, D_MODEL)
    cw = _compress_weights(w_cmp[0], pe_cmp[0])
    w_out_b = w_out[0].astype(BF16)
    g_post2 = g_post[0].reshape(1, D_MODEL)

    tabs_p = _rope_tables(jnp.arange(t_p, dtype=jnp.int32))
    xp2 = x_prompt.reshape(bp * t_p, D_MODEL)
    qk, v_m, o_m, z_m, q_n, q_rot, kvc, kvs, kvw, z_n, ms = _project(
        xp2, g_pre2, w_lp, w_hp, w_ms, tabs_p, t_p // PROJ_ROWS)
    r3 = lambda a: a.reshape(bp, t_p, a.shape[-1])
    qk3 = r3(qk)
    zeros = lambda *s: jnp.zeros(s, F32)
    h_m, c_p, n_p, m_p = _mlstm(qk3, r3(v_m), r3(ms), zeros(bp, SUBLANES, 2 * D_M),
                                zeros(bp, M_HEADS, M_HEAD_DIM, M_HEAD_DIM), zeros(bp, M_HEADS, M_HEAD_DIM),
                                zeros(bp, M_HEADS, LANES), w_conv[0], b_conv[0], b_igate[0], b_fgate[0],
                                MLSTM_CHUNK, MLSTM_CHUNK)
    kvc3, kvs3, kvw3 = r3(kvc), r3(kvs), r3(kvw)
    n_sub_p = t_p // D_CMP
    kv_c = _compress(kvc3.reshape(bp, n_sub_p, CMP_COLS), cw)
    n_sel_p = -(-t_p // L_SEL)
    o_rows = _nsa_prompt(
        _group_rows(r3(q_n), N_HEAD_DIM), _group_rows(r3(q_rot), N_HEAD_DIM),
        _group_rows(r3(ms)[:, :, 8:8 + 3 * N_HEADS], 3),
        _split_heads(kv_c, 0), jnp.swapaxes(_split_heads(kv_c, 1), 2, 3),
        _importance_matrix(n_sel_p, n_sub_p).astype(BF16),
        _split_heads(kvs3, 0).astype(BF16), _split_heads(kvs3, 1).astype(BF16),
        _split_heads(kvw3, 0).astype(BF16), _split_heads(kvw3, 1).astype(BF16),
        (jnp.arange(n_sel_p)[:, None] == (jnp.arange(t_p) // L_SEL)[None, :]).astype(BF16))
    o_n = _ungroup_rows(o_rows, bp, t_p).reshape(bp * t_p, D_N)
    y_p = _merge(xp2, h_m.reshape(bp * t_p, D_M), o_m, z_m, o_n, z_n, w_out_b, g_post2).reshape(bp, t_p, D_MODEL)
    kv_shape = lambda a, bb, tt: a.reshape(1, bb, tt, 2, KV_HEADS, N_HEAD_DIM)
    p_out = (kv_shape(kvc3, bp, t_p), kv_shape(kvs3, bp, t_p),
             kv_shape(kvw3[:, -min(WINDOW, t_p):], bp, min(WINDOW, t_p)),
             qk3[:, -(CONV_W - 1):][None], c_p[None], n_p[None], m_p[:, :, 0][None])

    pos_s = past + jnp.arange(t_s, dtype=jnp.int32)
    rows_s = bs * t_s
    tabs_s = tuple(jnp.tile(tb, (PROJ_ROWS // t_s, 1)) for tb in _rope_tables(pos_s))
    xs2 = x_sample.reshape(rows_s, D_MODEL)
    qk, v_m, o_m, z_m, q_n, q_rot, kvc, kvs, kvw, z_n, ms = _project(xs2, g_pre2, w_lp, w_hp, w_ms, tabs_s, 1)
    r3 = lambda a: a.reshape(bs, t_s, a.shape[-1])
    pad8 = lambda a: jnp.pad(r3(a), ((0, 0), (0, SUBLANES - t_s), (0, 0)))
    qk3 = r3(qk)
    tail = jnp.pad(state_conv[0], ((0, 0), (SUBLANES - (CONV_W - 1), 0), (0, 0)))
    m0 = jnp.broadcast_to(state_m[0][:, :, None], (bs, M_HEADS, LANES))
    h_m8, c_s, n_s, m_s = _mlstm(pad8(qk), pad8(v_m), pad8(ms), tail, state_C[0], state_n[0], m0,
                                 w_conv[0], b_conv[0], b_igate[0], b_fgate[0], SUBLANES, t_s)
    h_m = h_m8[:, :t_s].reshape(rows_s, D_M)
    conv_s = jnp.concatenate([state_conv[0], qk3], axis=1)[:, -(CONV_W - 1):]

    kvc3, kvs3 = r3(kvc), r3(kvs)
    n_sub_s = n_pages * (PAGE_SIZE // D_CMP)
    x_new = jnp.pad(kvc3, ((0, 0), (0, (-t_s) % D_CMP), (0, 0))).reshape(bs, CMP_COLS)
    fs_new = jnp.pad(_sub_proj(x_new, cw)[:, None, :], ((0, 0), (0, SUBLANES - 1), (0, 0)))
    n_sel_s = past // L_SEL + (-(-t_s // L_SEL))
    n_sel_pad = -(-n_sel_s // SUBLANES) * SUBLANES
    m_t = (_importance_matrix(n_sel_pad, n_sub_s) * (jnp.arange(n_sel_pad) < n_sel_s)[:, None]).astype(BF16)
    gq = (jnp.arange(GQA * t_s)[:, None] % t_s == jnp.arange(t_s)[None, :]).astype(F32)
    o_c, sel = _nsa_sample_cmp(page_table, _pages_as_stored(cache_kv_cmp[0]), cw, fs_new,
                               _sample_rows(r3(q_n), N_HEAD_DIM), m_t, gq, past, t_s)
    n_cols = KV_HEADS * GQA * t_s
    sel_t = jnp.broadcast_to(sel.transpose(0, 2, 1, 3)[:, :, :, None, :],
                             (bs, n_sel_pad, KV_HEADS, GQA, t_s)).reshape(bs, n_sel_pad, n_cols)
    qr_t = jnp.swapaxes(_sample_rows(r3(q_rot), N_HEAD_DIM), 1, 2)
    head_of_col = jnp.arange(n_cols) // (GQA * t_s)
    qbd_t = jnp.concatenate([jnp.where((head_of_col == hh)[None, None, :], qr_t, 0.0) for hh in range(KV_HEADS)],
                            axis=1)
    g_t = jnp.swapaxes(_sample_rows(r3(ms)[:, :, 8:8 + 3 * N_HEADS], 3), 1, 2)
    o_t, win_t = _nsa_sample_attn(page_table, _pages_as_stored(cache_kv_sel[0]), qbd_t, sel_t,
                                  _pages_as_stored(cache_kv_win[0]), pad8(kvs), pad8(kvw), o_c, g_t, past, t_s)
    o_n = o_t.reshape(bs, N_HEAD_DIM, N_HEADS, t_s).transpose(0, 3, 2, 1).reshape(rows_s, D_N)
    y_s = _merge(xs2, h_m, o_m, z_m, o_n, z_n, w_out_b, g_post2).reshape(bs, t_s, D_MODEL)
    s_out = (kv_shape(kvc3, bs, t_s), kv_shape(kvs3, bs, t_s),
             kv_shape(jnp.swapaxes(win_t, 1, 2), bs, WINDOW),
             conv_s[None], c_s[None], n_s[None], m_s[:, :, 0][None])

    return (y_p, y_s) + p_out + s_out
```

```python
import functools

import jax
import jax.numpy as jnp
from jax import lax
from jax.experimental import pallas as pl
from jax.experimental.pallas import tpu as pltpu

F32 = jnp.float32
BF16 = jnp.bfloat16
HI = lax.Precision.HIGHEST

D_MODEL = 1024
M_HEADS = 4
M_HEAD_DIM = 128
D_M = M_HEADS * M_HEAD_DIM
CONV_W = 4
N_HEADS = 8
N_HEAD_DIM = 64
D_N = N_HEADS * N_HEAD_DIM
KV_HEADS = 2
GQA = N_HEADS // KV_HEADS
D_CMP = 16
L_CMP = 32
L_SEL = 64
TOP_N = 16
WINDOW = 512
ROT_HALF = 8
ROPE_THETA = 500000.0
NORM_EPS = 1e-6
NEG = -1e30
FORCE = 1e4
KV_W = 2 * KV_HEADS * N_HEAD_DIM
PAGE_SIZE = 128
SLC_W = (1.0, 2.0, 2.0, 2.0, 1.0)
SEL_RATIO = L_SEL // D_CMP
QK_SCALE = N_HEAD_DIM ** -0.5
LOG2E = 1.4426950408889634

V7X_VMEM_BYTES = 64 * 1024 * 1024
VMEM_LIMIT = V7X_VMEM_BYTES - 8 * 1024 * 1024
LANES = 128
SUBLANES = 8

PROJ_ROWS = 256
Q_BLOCK = 128
KEY_TILE = 512
PROMPT_KEY_TILE = 1024
MLSTM_CHUNK = 256
CMP_COLS = D_CMP * KV_W
SUB_PITCH = D_CMP + 4


def _dot(a, b, prec=None):
    return jnp.dot(a, b, preferred_element_type=F32, precision=prec)


def _dot_nt(a, b, prec=None):
    return lax.dot_general(a, b, (((1,), (1,)), ((), ())), preferred_element_type=F32, precision=prec)


def _dot_tn(a, b, prec=None):
    return lax.dot_general(a, b, (((0,), (0,)), ((), ())), preferred_element_type=F32, precision=prec)


def _split_bf16(x):
    hi = x.astype(BF16)
    return hi, (x - hi.astype(F32)).astype(BF16)


def _dot_3pass(x, w_hi, w_lo):
    x_hi, x_lo = _split_bf16(x)
    return _dot(x_hi, w_hi) + _dot(x_lo, w_hi) + _dot(x_hi, w_lo)


def _dot_nt_3pass(a, b):
    a_hi, a_lo = _split_bf16(a)
    b_hi, b_lo = _split_bf16(b)
    return _dot_nt(a_hi, b_hi) + _dot_nt(a_lo, b_hi) + _dot_nt(a_hi, b_lo)


def _log_sigmoid(x):
    return jnp.minimum(x, 0.0) - jnp.log1p(jnp.exp(-jnp.abs(x)))


def _iota(shape, dim):
    return lax.broadcasted_iota(jnp.int32, shape, dim)


def _const_spec(a):
    nd = a.ndim
    return pl.BlockSpec(a.shape, lambda *_: (0,) * nd)


def _params(sem):
    return pltpu.CompilerParams(dimension_semantics=sem, vmem_limit_bytes=VMEM_LIMIT)


def _proj_kernel(x_ref, g_ref, wlp_ref, whp_ref, wms_ref, ra_ref, rb_ref, rc_ref,
                 qk_ref, v_ref, o_ref, zm_ref, q_ref, qr_ref, kvc_ref, kvs_ref, kvw_ref, zn_ref, ms_ref):
    x = x_ref[...]
    xn = x * lax.rsqrt(jnp.mean(x * x, axis=-1, keepdims=True) + NORM_EPS) * g_ref[...]
    xb = xn.astype(BF16)
    ra, rb, rc = ra_ref[...], rb_ref[...], rc_ref[...]

    def rope(a):
        return a * ra + pltpu.roll(a, LANES - ROT_HALF, 1) * rb + pltpu.roll(a, ROT_HALF, 1) * rc

    def lp(lo, hi):
        return jnp.dot(xb, wlp_ref[:, lo:hi], preferred_element_type=F32)

    qk_ref[...] = lp(0, 1024)
    v_ref[...] = lp(1024, 1536)
    o_ref[...] = lp(1536, 2048)
    zm_ref[...] = lp(2048, 2560)
    kvs = lp(2560, 2816)
    kvs_ref[:, :LANES] = rope(kvs[:, :LANES])
    kvs_ref[:, LANES:] = kvs[:, LANES:]
    kvw = lp(2816, 3072)
    kvw_ref[:, :LANES] = rope(kvw[:, :LANES])
    kvw_ref[:, LANES:] = kvw[:, LANES:]
    zn_ref[...] = lp(3072, 3584)
    hp = jnp.dot(xn, whp_ref[...], preferred_element_type=F32, precision=HI)
    q_ref[...] = hp[:, :D_N]
    for c in range(D_N // LANES):
        qr_ref[:, c * LANES:(c + 1) * LANES] = rope(hp[:, c * LANES:(c + 1) * LANES])
    kvc_ref[...] = hp[:, D_N:]
    ms_ref[...] = jnp.dot(xn, wms_ref[...], preferred_element_type=F32, precision=HI)


def _rope_tables(pos):
    inv = ROPE_THETA ** (-jnp.arange(ROT_HALF, dtype=F32) / ROT_HALF)
    ang = pos.astype(F32)[:, None] * inv[None, :]
    cos, sin = jnp.cos(ang), jnp.sin(ang)
    one = jnp.ones((pos.shape[0], N_HEAD_DIM - 2 * ROT_HALF), F32)
    zero8 = jnp.zeros_like(cos)
    zero = jnp.zeros_like(one)
    ra = jnp.concatenate([cos, cos, one], axis=1)
    rb = jnp.concatenate([-sin, zero8, zero], axis=1)
    rc = jnp.concatenate([zero8, sin, zero], axis=1)
    return tuple(jnp.tile(t, (1, LANES // N_HEAD_DIM)) for t in (ra, rb, rc))


def _project(x2d, g_pre, w_lp, w_hp, w_ms, tabs, tab_blocks):
    n = x2d.shape[0]
    r = PROJ_ROWS
    row = lambda w: pl.BlockSpec((r, w), lambda i: (i, 0))
    tab = pl.BlockSpec((r, LANES), lambda i: (i % tab_blocks, 0))
    widths = (1024, 512, 512, 512, 512, 512, 256, 256, 256, 512, 128)
    return pl.pallas_call(
        _proj_kernel,
        grid=(n // r,),
        in_specs=[row(D_MODEL), _const_spec(g_pre), _const_spec(w_lp), _const_spec(w_hp), _const_spec(w_ms), tab, tab, tab],
        out_specs=[row(w) for w in widths],
        out_shape=[jax.ShapeDtypeStruct((n, w), F32) for w in widths],
        compiler_params=_params(("parallel",)),
        name="project",
    )(x2d, g_pre, w_lp, w_hp, w_ms, *tabs)


def _mlstm_kernel(qk_ref, v_ref, ms_ref, mst_ref, tail_ref, c0_ref, n0_ref, m0_ref, wc_ref, bc_ref, gbr_ref, gbc_ref,
                  h_ref, c_ref, n_ref, m_ref, xp_sc, c_sc, n_sc, m_sc, *, chunk, n_valid):
    L = chunk
    ci = pl.program_id(1)

    @pl.when(ci == 0)
    def _():
        xp_sc[0:SUBLANES, :] = tail_ref[0]
        c_sc[...] = c0_ref[0]
        n_sc[0:M_HEADS, :] = n0_ref[0]
        m_sc[0:M_HEADS, :] = m0_ref[0]

    xp_sc[SUBLANES:SUBLANES + L, :] = qk_ref[0]
    y = bc_ref[...]
    for j in range(CONV_W):
        y = y + xp_sc[pl.ds(SUBLANES - (CONV_W - 1) + j, L), :] * wc_ref[j:j + 1, :]
    qk = y * jax.nn.sigmoid(y)
    xp_sc[0:SUBLANES, :] = xp_sc[L:L + SUBLANES, :]

    row = _iota((L, L), 0)
    col = _iota((L, L), 1)
    tri = row >= col
    g = ms_ref[0] + gbr_ref[...]
    fl = _log_sigmoid(g)
    g_t = mst_ref[0] + gbc_ref[...]
    fl_t = _log_sigmoid(g_t)
    if n_valid < L:
        ok = _iota((L, LANES), 0) < n_valid
        g = jnp.where(ok, g, NEG)
        fl = jnp.where(ok, fl, 0.0)
        ok_t = _iota((SUBLANES, L), 1) < n_valid
        g_t = jnp.where(ok_t, g_t, NEG)
        fl_t = jnp.where(ok_t, fl_t, 0.0)
    cum = _dot(jnp.where(tri, 1.0, 0.0), fl, HI)
    cum_t = _dot(fl_t, jnp.where(row <= col, 1.0, 0.0), HI)

    for h in range(M_HEADS):
        hs = slice(h * M_HEAD_DIM, (h + 1) * M_HEAD_DIM)
        q = qk[:, hs]
        k = qk[:, D_M + h * M_HEAD_DIM:D_M + (h + 1) * M_HEAD_DIM] * (M_HEAD_DIM ** -0.5)
        v = v_ref[0, :, hs]
        ic = g[:, h:h + 1]
        bcol = cum[:, M_HEADS + h:M_HEADS + h + 1]
        ir = g_t[h:h + 1, :]
        brow = cum_t[M_HEADS + h:M_HEADS + h + 1, :]
        m_prev = m_sc[h:h + 1, 0:1]
        c_old = c_sc[h]
        n_old = n_sc[h:h + 1, :]
        log_d = jnp.where(tri, bcol - brow + ir, NEG)
        m_inter = bcol + m_prev
        m_new = jnp.maximum(m_inter, jnp.max(log_d, axis=1, keepdims=True))
        w = _dot_nt(q, k, HI) * jnp.exp(log_d - m_new)
        inter = jnp.exp(m_inter - m_new)
        num = _dot(w, v, HI) + inter * _dot_nt(q, c_old, HI)
        nq = jnp.sum(w, axis=1, keepdims=True) + inter * jnp.sum(q * n_old, axis=1, keepdims=True)
        h_ref[0, :, hs] = num / jnp.maximum(jnp.abs(nq), jnp.exp(-m_new))
        m_end = m_new[L - 1:L, :]
        b_end = bcol[L - 1:L, :]
        decay = jnp.exp(b_end + m_prev - m_end)
        wk = jnp.exp(b_end - bcol + ic - m_end)
        c_sc[h] = decay * c_old + _dot_tn(wk * v, k, HI)
        n_sc[h:h + 1, :] = decay * n_old + jnp.sum(wk * k, axis=0, keepdims=True)
        m_sc[h:h + 1, :] = jnp.broadcast_to(m_end, (1, LANES))

    @pl.when(ci == pl.num_programs(1) - 1)
    def _():
        c_ref[0] = c_sc[...]
        n_ref[0] = n_sc[0:M_HEADS, :]
        m_ref[0] = m_sc[0:M_HEADS, :]


def _mlstm(qk, v, ms, tail, c0, n0, m0, w_conv, b_conv, b_igate, b_fgate, chunk, n_valid):
    b, t, _ = qk.shape
    nc = t // chunk
    gb = jnp.concatenate([b_igate, b_fgate])
    gb_row = jnp.zeros((1, LANES), F32).at[0, :2 * M_HEADS].set(gb)
    gb_col = gb.reshape(2 * M_HEADS, 1)
    ms_t = jnp.swapaxes(ms[:, :, :2 * M_HEADS], 1, 2)
    seq = lambda w: pl.BlockSpec((1, chunk, w), lambda i, c: (i, c, 0))
    per_b = lambda a: pl.BlockSpec((1,) + a.shape[1:], lambda i, c: (i,) + (0,) * (a.ndim - 1))
    kern = functools.partial(_mlstm_kernel, chunk=chunk, n_valid=n_valid)
    out_shape = [jax.ShapeDtypeStruct((b, t, D_M), F32),
                 jax.ShapeDtypeStruct((b, M_HEADS, M_HEAD_DIM, M_HEAD_DIM), F32),
                 jax.ShapeDtypeStruct((b, M_HEADS, M_HEAD_DIM), F32),
                 jax.ShapeDtypeStruct((b, M_HEADS, LANES), F32)]
    return pl.pallas_call(
        kern,
        grid=(b, nc),
        in_specs=[seq(2 * D_M), seq(D_M), seq(LANES),
                  pl.BlockSpec((1, 2 * M_HEADS, chunk), lambda i, c: (i, 0, c)),
                  per_b(tail), per_b(c0), per_b(n0), per_b(m0),
                  _const_spec(w_conv), _const_spec(b_conv.reshape(1, -1)), _const_spec(gb_row), _const_spec(gb_col)],
        out_specs=[seq(D_M), per_b(out_shape[1]), per_b(out_shape[2]), per_b(out_shape[3])],
        out_shape=out_shape,
        scratch_shapes=[pltpu.VMEM((chunk + SUBLANES, 2 * D_M), F32),
                        pltpu.VMEM((M_HEADS, M_HEAD_DIM, M_HEAD_DIM), F32),
                        pltpu.VMEM((SUBLANES, M_HEAD_DIM), F32),
                        pltpu.VMEM((SUBLANES, LANES), F32)],
        compiler_params=_params(("parallel", "arbitrary")),
        name="mlstm",
    )(qk, v, ms, ms_t, tail, c0, n0, m0, w_conv, b_conv.reshape(1, -1), gb_row, gb_col)


def _sub_block_projections(xk_of_p, xv_of_p, wk_hi_ref, wk_lo_ref, wv_ref):
    acc_k = acc_v = None
    for pp in range(D_CMP // 2):
        rs = slice(pp * 2 * LANES, (pp + 1) * 2 * LANES)
        xk = jnp.concatenate([xk_of_p(2 * pp), xk_of_p(2 * pp + 1)], axis=1)
        xv = jnp.concatenate([xv_of_p(2 * pp), xv_of_p(2 * pp + 1)], axis=1)
        tk = _dot_3pass(xk, wk_hi_ref[rs, :], wk_lo_ref[rs, :])
        tv = _dot(xv.astype(BF16), wv_ref[rs, :])
        acc_k = tk if acc_k is None else acc_k + tk
        acc_v = tv if acc_v is None else acc_v + tv
    return jnp.concatenate([acc_k, acc_v], axis=1)


def _compress_bias(pek_ref, pev_ref, wk_hi_ref, wk_lo_ref, wv_ref):
    b8 = _sub_block_projections(lambda p: pek_ref[:, p * LANES:(p + 1) * LANES],
                                lambda p: pev_ref[:, p * LANES:(p + 1) * LANES], wk_hi_ref, wk_lo_ref, wv_ref)
    return (b8[0:1, 0:LANES] + b8[1:2, LANES:2 * LANES],
            b8[0:1, 2 * LANES:3 * LANES] + b8[1:2, 3 * LANES:4 * LANES])


def _combine_sub_blocks(fs_sc, n_sub, bias_k, bias_v):
    k = fs_sc[0:n_sub, 0:LANES] + fs_sc[pl.ds(1, n_sub), LANES:2 * LANES] + bias_k
    v = fs_sc[0:n_sub, 2 * LANES:3 * LANES] + fs_sc[pl.ds(1, n_sub), 3 * LANES:4 * LANES] + bias_v
    return k, v


def _compress_kernel(x_ref, wk_hi_ref, wk_lo_ref, wv_ref, pek_ref, pev_ref, o_ref, fs_sc):
    n_sub = x_ref.shape[1]
    w = (wk_hi_ref, wk_lo_ref, wv_ref)
    fs_sc[0:n_sub, :] = _sub_block_projections(
        lambda p: x_ref[0, :, p * KV_W:p * KV_W + LANES],
        lambda p: x_ref[0, :, p * KV_W + LANES:(p + 1) * KV_W], *w)
    fs_sc[n_sub:n_sub + SUBLANES, :] = jnp.zeros((SUBLANES, 2 * KV_W), F32)
    k, v = _combine_sub_blocks(fs_sc, n_sub, *_compress_bias(pek_ref, pev_ref, *w))
    o_ref[0, :, 0:LANES] = k
    o_ref[0, :, LANES:] = v


def _compress(x, cw):
    b, n_sub, _ = x.shape
    return pl.pallas_call(
        _compress_kernel,
        grid=(b,),
        in_specs=[pl.BlockSpec((1, n_sub, CMP_COLS), lambda i: (i, 0, 0))] + [_const_spec(a) for a in cw],
        out_specs=pl.BlockSpec((1, n_sub, KV_W), lambda i: (i, 0, 0)),
        out_shape=jax.ShapeDtypeStruct((b, n_sub, KV_W), F32),
        scratch_shapes=[pltpu.VMEM((n_sub + SUBLANES, 2 * KV_W), F32)],
        compiler_params=_params(("parallel",)),
        name="compress",
    )(x, *cw)


def _sub_proj_kernel(x_ref, wk_hi_ref, wk_lo_ref, wv_ref, o_ref):
    o_ref[...] = _sub_block_projections(
        lambda p: x_ref[:, p * KV_W:p * KV_W + LANES],
        lambda p: x_ref[:, p * KV_W + LANES:(p + 1) * KV_W], wk_hi_ref, wk_lo_ref, wv_ref)


def _sub_proj(x, cw):
    return pl.pallas_call(
        _sub_proj_kernel,
        out_shape=jax.ShapeDtypeStruct((x.shape[0], 2 * KV_W), F32),
        compiler_params=pltpu.CompilerParams(vmem_limit_bytes=VMEM_LIMIT),
        name="sub_proj_new",
    )(x, *cw[:3])


def _compress_weights(w_cmp, pe_cmp):
    w5 = w_cmp.reshape(2, 2, D_CMP, N_HEAD_DIM, N_HEAD_DIM)
    eye = jnp.eye(KV_HEADS, dtype=F32)
    stack = lambda wc: jnp.einsum('fpde,hH->phdfHe', wc, eye).reshape(D_CMP * LANES, 2 * LANES)
    wk_hi, wk_lo = _split_bf16(stack(w5[0]))
    wv = stack(w5[1]).astype(BF16)
    pe4 = pe_cmp.reshape(2, 2, D_CMP, N_HEAD_DIM)
    rows = lambda pc: jnp.zeros((SUBLANES, D_CMP * LANES), F32).at[0:2].set(
        jnp.broadcast_to(pc[:, :, None, :], (2, D_CMP, KV_HEADS, N_HEAD_DIM)).reshape(2, D_CMP * LANES))
    return wk_hi, wk_lo, wv, rows(pe4[0]), rows(pe4[1])


def _importance_matrix(n_sel_pad, n_cmp_pad):
    j = jnp.arange(n_sel_pad)[:, None]
    n = jnp.arange(n_cmp_pad)[None, :]
    r = n - SEL_RATIO * j + 1
    w = jnp.asarray(SLC_W, F32)
    return jnp.where((r >= 0) & (r < len(SLC_W)), w[jnp.clip(r, 0, len(SLC_W) - 1)], 0.0).astype(F32)


def _importance(mt_bf16, p_sum):
    p_hi, p_lo = _split_bf16(p_sum)
    return _dot(mt_bf16, p_hi) + _dot(mt_bf16, p_lo)


def _rank_select(v, n_cand):
    n_rows = v.shape[0]
    slabs = [v[r:r + SUBLANES, :] for r in range(0, n_rows, SUBLANES)]
    cnts = [jnp.zeros(s.shape, F32) for s in slabs]
    j_in = _iota(slabs[0].shape, 0)
    for jp in range(n_cand):
        rowv = slabs[jp // SUBLANES][jp % SUBLANES:jp % SUBLANES + 1, :]
        for si, s in enumerate(slabs):
            if si * SUBLANES > jp:
                beats = rowv >= s
            elif si * SUBLANES + SUBLANES - 1 < jp:
                beats = rowv > s
            else:
                beats = (rowv > s) | ((rowv == s) & (j_in > jp - si * SUBLANES))
            cnts[si] = cnts[si] + jnp.where(beats, 1.0, 0.0)
    return jnp.concatenate([jnp.where(c < TOP_N, 1.0, 0.0) for c in cnts], axis=0)


def _rank_select_lanes(v, n_cand):
    chunks = [v[:, c:c + LANES] for c in range(0, v.shape[1], LANES)]
    cnts = [jnp.zeros(c.shape, F32) for c in chunks]
    j_in = _iota(chunks[0].shape, 1)
    for jp in range(n_cand):
        colv = chunks[jp // LANES][:, jp % LANES:jp % LANES + 1]
        for ci, c in enumerate(chunks):
            if ci * LANES > jp:
                beats = colv >= c
            elif ci * LANES + LANES - 1 < jp:
                beats = colv > c
            else:
                beats = (colv > c) | ((colv == c) & (j_in > jp - ci * LANES))
            cnts[ci] = cnts[ci] + jnp.where(beats, 1.0, 0.0)
    return jnp.concatenate([jnp.where(c < TOP_N, 1.0, 0.0) for c in cnts], axis=1)


def _selection_bonus(j, t):
    cur = t // L_SEL
    forced = (j == 0) | (j == cur) | (j == cur - 1)
    return jnp.where(j > cur, NEG, jnp.where(forced, FORCE, 0.0))


def _nsa_prompt_kernel(q_ref, qr_ref, g_ref, kc_ref, vct_ref, mt_ref, ka_ref, va_ref, kw_ref, vw_ref,
                       o_ref, m_sc, acc_sc):
    qb = pl.program_id(2)
    start = qb * Q_BLOCK
    rows = GQA * Q_BLOCK
    n_cmp = kc_ref.shape[2]
    n_sel = mt_ref.shape[0]

    q = q_ref[0, 0, 0] * QK_SCALE
    s_t = _dot_nt_3pass(kc_ref[0, 0], q)
    n_i = _iota((n_cmp, rows), 0)
    t_l = start + (_iota((n_cmp, rows), 1) & (Q_BLOCK - 1))
    ok = (D_CMP * n_i + (L_CMP - 1)) <= t_l
    s_t = jnp.where(ok, s_t, NEG)
    p_t = jnp.where(ok, jnp.exp(s_t - jnp.max(s_t, axis=0, keepdims=True)), 0.0)
    l_t = jnp.sum(p_t, axis=0, keepdims=True)
    p_t = p_t / jnp.where(l_t > 0.0, l_t, 1.0)
    o_c = _dot(vct_ref[0, 0].astype(BF16), p_t.astype(BF16)).T

    p_sum = p_t[:, 0:Q_BLOCK]
    for gi in range(1, GQA):
        p_sum = p_sum + p_t[:, gi * Q_BLOCK:(gi + 1) * Q_BLOCK]
    imp_t = _importance(mt_ref[...], p_sum)
    j = _iota((n_sel, Q_BLOCK), 0)
    t_q = start + _iota((n_sel, Q_BLOCK), 1)
    sel = _rank_select(imp_t + _selection_bonus(j, t_q), n_sel).T

    qr = (qr_ref[0, 0, 0] * (QK_SCALE * LOG2E)).astype(BF16)
    block_bias = ((sel - 1.0) * -NEG).astype(BF16)
    q_aug = jnp.concatenate([jnp.concatenate([block_bias] * GQA, axis=0), qr, jnp.zeros_like(qr)], axis=1)
    m_sc[...] = jnp.full(m_sc.shape, NEG, F32)
    acc_sc[...] = jnp.zeros(acc_sc.shape, F32)

    kt_keys = PROMPT_KEY_TILE

    def tile(k0, causal):
        s = _dot_nt(q_aug, ka_ref[0, 0, pl.ds(k0, kt_keys), :])
        if causal:
            t_r = start + (_iota((rows, kt_keys), 0) & (Q_BLOCK - 1))
            s = jnp.where((k0 + _iota((rows, kt_keys), 1)) <= t_r, s, NEG)
        m_old = m_sc[...]
        m_new = jnp.maximum(m_old, jnp.max(s, axis=1, keepdims=True))
        p = jnp.exp2(s - pltpu.repeat(m_new, kt_keys // LANES, axis=1))
        acc_sc[...] = jnp.exp2(m_old - m_new) * acc_sc[...] + _dot(p.astype(BF16), va_ref[0, 0, pl.ds(k0, kt_keys), :])
        m_sc[...] = m_new

    n_full = start // kt_keys

    def full_tile(kt, carry):
        tile(pl.multiple_of(kt * kt_keys, kt_keys), False)
        return carry

    lax.fori_loop(0, n_full, full_tile, 0)
    tile(pl.multiple_of(n_full * kt_keys, kt_keys), True)
    o_s = acc_sc[:, 0:N_HEAD_DIM] / acc_sc[:, N_HEAD_DIM:N_HEAD_DIM + 1]

    span = WINDOW + Q_BLOCK
    w0 = pl.multiple_of(jnp.maximum(start - WINDOW, 0), Q_BLOCK)
    s = _dot_nt(qr, kw_ref[0, 0, pl.ds(w0, span), :])
    dt = (start + (_iota((rows, span), 0) & (Q_BLOCK - 1))) - (w0 + _iota((rows, span), 1))
    s = jnp.where((dt >= 0) & (dt < WINDOW), s, NEG)
    p = jnp.exp2(s - jnp.max(s, axis=1, keepdims=True))
    o_w = _dot(p.astype(BF16), vw_ref[0, 0, pl.ds(w0, span), :]) / jnp.sum(p, axis=1, keepdims=True)

    gate = jax.nn.sigmoid(g_ref[0, 0, 0])
    o_ref[0, 0, 0] = gate[:, 0:1] * o_c + gate[:, 1:2] * o_s + gate[:, 2:3] * o_w


def _nsa_prompt(q_r, qrot_r, g_r, k_c, v_ct, m_t, k_aug, v_aug, k_w, v_w):
    b, _, nqb, rows, _ = q_r.shape
    blk = lambda w: pl.BlockSpec((1, 1, 1, rows, w), lambda i, h, c: (i, h, c, 0, 0))
    per_bh = lambda a: pl.BlockSpec((1, 1) + a.shape[2:], lambda i, h, c: (i, h, 0, 0))
    return pl.pallas_call(
        _nsa_prompt_kernel,
        grid=(b, KV_HEADS, nqb),
        in_specs=[blk(N_HEAD_DIM), blk(N_HEAD_DIM), blk(3), per_bh(k_c), per_bh(v_ct), _const_spec(m_t),
                  per_bh(k_aug), per_bh(v_aug), per_bh(k_w), per_bh(v_w)],
        out_specs=blk(N_HEAD_DIM),
        out_shape=jax.ShapeDtypeStruct((b, KV_HEADS, nqb, rows, N_HEAD_DIM), F32),
        scratch_shapes=[pltpu.VMEM((rows, LANES), F32), pltpu.VMEM((rows, LANES), F32)],
        compiler_params=_params(("parallel", "parallel", "arbitrary")),
        name="nsa_prompt",
    )(q_r, qrot_r, g_r, k_c, v_ct, m_t, k_aug, v_aug, k_w, v_w)


def _gather_pages(pt_ref, cache_hbm, buf, sem, n_pages):
    b = pl.program_id(0)
    slot = b % 2

    def page_copy(seq, j, s):
        return pltpu.make_async_copy(cache_hbm.at[pt_ref[seq, j]], buf.at[s, j], sem.at[s])

    def issue(seq, s):
        def body(j, c):
            page_copy(seq, j, s).start()
            return c
        lax.fori_loop(0, n_pages, body, 0)

    @pl.when(b == 0)
    def _():
        issue(0, 0)

    @pl.when(b + 1 < pl.num_programs(0))
    def _():
        issue(b + 1, 1 - slot)

    def wait_body(j, c):
        page_copy(b, j, slot).wait()
        return c
    lax.fori_loop(0, n_pages, wait_body, 0)
    return slot


def _nsa_sample_cmp_kernel(pt_ref, cache_hbm, wk_hi_ref, wk_lo_ref, wv_ref, pek_ref, pev_ref, fsn_ref, q_ref,
                           m_ref, gq_ref, oc_ref, sel_ref, xbuf, xk_sc, xv_sc, fs_sc, sem,
                           *, n_pages, past, t_new, n_sel):
    sub_per_page = PAGE_SIZE // D_CMP
    n_sub = n_pages * sub_per_page
    half = KV_W // 2
    slot = _gather_pages(pt_ref, cache_hbm, xbuf, sem, n_pages)

    for j in range(n_pages):
        rows_k = xbuf[slot, j, 0:half, :].T
        rows_v = xbuf[slot, j, half:KV_W, :].T
        for n in range(sub_per_page):
            r0 = (j * sub_per_page + n) * SUB_PITCH
            xk_sc[r0:r0 + D_CMP, :] = rows_k[n * D_CMP:(n + 1) * D_CMP, :]
            xv_sc[r0:r0 + D_CMP, :] = rows_v[n * D_CMP:(n + 1) * D_CMP, :]

    w = (wk_hi_ref, wk_lo_ref, wv_ref)
    fs_sc[0:n_sub, :] = _sub_block_projections(lambda p: xk_sc[pl.ds(p, n_sub, stride=SUB_PITCH), :],
                                               lambda p: xv_sc[pl.ds(p, n_sub, stride=SUB_PITCH), :], *w)
    fs_sc[n_sub:n_sub + SUBLANES, :] = fsn_ref[0]
    kc, vc = _combine_sub_blocks(fs_sc, n_sub, *_compress_bias(pek_ref, pev_ref, *w))

    rows = GQA * t_new
    p_sum = None
    for h in range(KV_HEADS):
        k_h = kc[:, h * N_HEAD_DIM:(h + 1) * N_HEAD_DIM]
        v_h = vc[:, h * N_HEAD_DIM:(h + 1) * N_HEAD_DIM]
        q_h = q_ref[0, h * rows:(h + 1) * rows, :] * QK_SCALE
        s_t = _dot_nt_3pass(k_h, q_h)
        t_l = past + lax.rem(_iota((n_sub, rows), 1), t_new)
        ok = (D_CMP * _iota((n_sub, rows), 0) + (L_CMP - 1)) <= t_l
        s_t = jnp.where(ok, s_t, NEG)
        p_t = jnp.where(ok, jnp.exp(s_t - jnp.max(s_t, axis=0, keepdims=True)), 0.0)
        l_t = jnp.sum(p_t, axis=0, keepdims=True)
        p_t = p_t / jnp.where(l_t > 0.0, l_t, 1.0)
        ps = _dot(p_t, gq_ref[h], HI)
        p_sum = ps if p_sum is None else p_sum + ps
        oc_ref[0, h * rows:(h + 1) * rows, :] = _dot_tn(p_t.astype(BF16), v_h.astype(BF16))
    p_hi, p_lo = _split_bf16(p_sum)
    imp = _dot_tn(p_hi, m_ref[...]) + _dot_tn(p_lo, m_ref[...])
    t_q = past + lax.rem(_iota(imp.shape, 0), t_new)
    sel_ref[0] = _rank_select_lanes(imp + _selection_bonus(_iota(imp.shape, 1), t_q), n_sel)


def _nsa_sample_cmp(page_table, cache_t, cw, fs_new, q_rows, m_mat, gq, past, t_new, n_sel):
    b, n_pages = page_table.shape
    n_sub = n_pages * (PAGE_SIZE // D_CMP)
    rows = q_rows.shape[1]
    kern = functools.partial(_nsa_sample_cmp_kernel, n_pages=n_pages, past=past, t_new=t_new, n_sel=n_sel)
    const = lambda a: pl.BlockSpec(a.shape, lambda i, pt: (0,) * a.ndim)
    per_b = lambda a: pl.BlockSpec((1,) + a.shape[1:], lambda i, pt: (i,) + (0,) * (a.ndim - 1))
    out_shape = [jax.ShapeDtypeStruct((b, rows, N_HEAD_DIM), F32),
                 jax.ShapeDtypeStruct((b, KV_HEADS * t_new, m_mat.shape[1]), F32)]
    return pl.pallas_call(
        kern,
        grid_spec=pltpu.PrefetchScalarGridSpec(
            num_scalar_prefetch=1,
            grid=(b,),
            in_specs=[pl.BlockSpec(memory_space=pl.ANY)] + [const(a) for a in cw]
                     + [per_b(fs_new), per_b(q_rows), const(m_mat), const(gq)],
            out_specs=[per_b(out_shape[0]), per_b(out_shape[1])],
            scratch_shapes=[pltpu.VMEM((2, n_pages, KV_W, PAGE_SIZE), F32),
                            pltpu.VMEM((n_sub * SUB_PITCH, KV_W // 2), F32),
                            pltpu.VMEM((n_sub * SUB_PITCH, KV_W // 2), F32),
                            pltpu.VMEM((n_sub + SUBLANES, 2 * KV_W), F32),
                            pltpu.SemaphoreType.DMA((2,))]),
        out_shape=out_shape,
        compiler_params=_params(("arbitrary",)),
        name="nsa_sample_cmp",
    )(page_table, cache_t, *cw, fs_new, q_rows, m_mat, gq)


def _nsa_sample_attn_kernel(pt_ref, cache_hbm, qbd_ref, sel_ref, win_ref, kvsn_ref, kvwn_ref, oc_ref, g_ref,
                            o_ref, wout_ref, kbuf, sem, *, n_pages, past, t_new):
    slot = _gather_pages(pt_ref, cache_hbm, kbuf, sem, n_pages)
    rows = qbd_ref.shape[1]
    half = KV_W // 2
    qbd = (qbd_ref[0] * (QK_SCALE * LOG2E)).astype(BF16)
    sel = sel_ref[0]
    n_buf = win_ref.shape[2]
    pages_per_tile = KEY_TILE // PAGE_SIZE
    blocks_per_tile = KEY_TILE // L_SEL
    n_pb = past // L_SEL
    t_r = past + lax.rem(_iota((rows, 1), 0), t_new)

    def new_rows(ref, extra_ok):
        kn = ref[0]
        s = _dot_nt(qbd, kn[:, :half].astype(BF16))
        ki = _iota((rows, SUBLANES), 1)
        ok = (ki < t_new) & ((past + ki) <= t_r) & extra_ok
        return jnp.where(ok, s, NEG), lambda p: _dot(p, kn[:, half:].astype(BF16))

    def softmax_attend(parts):
        m = jnp.max(parts[0][0], axis=1, keepdims=True)
        for s, _ in parts[1:]:
            m = jnp.maximum(m, jnp.max(s, axis=1, keepdims=True))
        l = jnp.zeros((rows, 1), F32)
        acc = jnp.zeros((rows, half), F32)
        for s, weighted_values in parts:
            p = jnp.where(s > 0.5 * NEG, jnp.exp2(s - m), 0.0)
            l = l + jnp.sum(p, axis=1, keepdims=True)
            acc = acc + weighted_values(p.astype(BF16))
        own = jnp.where(_iota((rows, N_HEAD_DIM), 0) < rows // KV_HEADS, acc[:, :N_HEAD_DIM], acc[:, N_HEAD_DIM:])
        return own / l

    parts = []
    for c in range(n_pages // pages_per_tile):
        pages = [kbuf[slot, c * pages_per_tile + i] for i in range(pages_per_tile)]
        k_t = jnp.concatenate([pg[0:half, :] for pg in pages], axis=1).astype(BF16)
        v_t = jnp.concatenate([pg[half:KV_W, :] for pg in pages], axis=1).astype(BF16)
        picked = jnp.concatenate(
            [jnp.broadcast_to(sel[:, c * blocks_per_tile + i:c * blocks_per_tile + i + 1], (rows, L_SEL))
             for i in range(blocks_per_tile)], axis=1)
        parts.append((jnp.where(picked > 0.5, _dot(qbd, k_t), NEG), lambda p, v_t=v_t: _dot_nt(p, v_t)))
    o_s = softmax_attend(parts + [new_rows(kvsn_ref, sel[:, n_pb:n_pb + 1] > 0.5)])

    win = win_ref[0]
    dt = t_r - (past - n_buf + _iota((rows, n_buf), 1))
    s_w = jnp.where((dt >= 0) & (dt < WINDOW), _dot(qbd, win[:half, :].astype(BF16)), NEG)
    o_w = softmax_attend([(s_w, lambda p: _dot_nt(p, win[half:, :].astype(BF16))), new_rows(kvwn_ref, True)])

    gate = jax.nn.sigmoid(g_ref[0])
    o_ref[0] = gate[:, 0:1] * oc_ref[0] + gate[:, 1:2] * o_s + gate[:, 2:3] * o_w

    rolled = pltpu.roll(win, n_buf - t_new, 1)
    place = jnp.where((_iota((SUBLANES, LANES), 1) == (LANES - t_new) + _iota((SUBLANES, LANES), 0))
                      & (_iota((SUBLANES, LANES), 0) < t_new), 1.0, 0.0)
    tail = _dot_tn(kvwn_ref[0], place, HI)
    wout_ref[0, :, 0:n_buf - LANES] = rolled[:, 0:n_buf - LANES]
    wout_ref[0, :, n_buf - LANES:] = jnp.where(_iota((KV_W, LANES), 1) >= LANES - t_new, tail,
                                               rolled[:, n_buf - LANES:])


def _nsa_sample_attn(page_table, cache_t, qbd, sel, win_t, kvs_new, kvw_new, o_c, g_rows, past, t_new):
    b, n_pages = page_table.shape
    kern = functools.partial(_nsa_sample_attn_kernel, n_pages=n_pages, past=past, t_new=t_new)
    per_b = lambda a: pl.BlockSpec((1,) + a.shape[1:], lambda i, pt: (i,) + (0,) * (a.ndim - 1))
    out_shape = [jax.ShapeDtypeStruct(o_c.shape, F32), jax.ShapeDtypeStruct(win_t.shape, F32)]
    return pl.pallas_call(
        kern,
        grid_spec=pltpu.PrefetchScalarGridSpec(
            num_scalar_prefetch=1,
            grid=(b,),
            in_specs=[pl.BlockSpec(memory_space=pl.ANY), per_b(qbd), per_b(sel), per_b(win_t), per_b(kvs_new),
                      per_b(kvw_new), per_b(o_c), per_b(g_rows)],
            out_specs=[per_b(o_c), per_b(win_t)],
            scratch_shapes=[pltpu.VMEM((2, n_pages, KV_W, PAGE_SIZE), F32),
                            pltpu.SemaphoreType.DMA((2,))]),
        out_shape=out_shape,
        compiler_params=_params(("arbitrary",)),
        name="nsa_sample_attn",
    )(page_table, cache_t, qbd, sel, win_t, kvs_new, kvw_new, o_c, g_rows)


def _merge_kernel(x_ref, h_ref, om_ref, zm_ref, on_ref, zn_ref, w_ref, g_ref, y_ref):
    zm = zm_ref[...]
    zn = zn_ref[...]
    a = h_ref[...] * jax.nn.sigmoid(om_ref[...]) * (zm * jax.nn.sigmoid(zm))
    c = on_ref[...] * (zn * jax.nn.sigmoid(zn))
    y = _dot(a.astype(BF16), w_ref[0:D_M, :]) + _dot(c.astype(BF16), w_ref[D_M:, :])
    yn = y * lax.rsqrt(jnp.mean(y * y, axis=-1, keepdims=True) + NORM_EPS)
    y_ref[...] = x_ref[...] + yn * g_ref[...]


def _merge(x2d, h_m, o_m, z_m, o_n, z_n, w_out, g_post):
    n = x2d.shape[0]
    r = PROJ_ROWS
    row = lambda w: pl.BlockSpec((r, w), lambda i: (i, 0))
    return pl.pallas_call(
        _merge_kernel,
        grid=(n // r,),
        in_specs=[row(D_MODEL), row(D_M), row(D_M), row(D_M), row(D_N), row(D_N), _const_spec(w_out), _const_spec(g_post)],
        out_specs=row(D_MODEL),
        out_shape=jax.ShapeDtypeStruct((n, D_MODEL), F32),
        compiler_params=_params(("parallel",)),
        name="merge",
    )(x2d, h_m, o_m, z_m, o_n, z_n, w_out, g_post)


def _split_heads(kv, c):
    b, t, _ = kv.shape
    return kv.reshape(b, t, 2, KV_HEADS, N_HEAD_DIM)[:, :, c].transpose(0, 2, 1, 3)


def _group_rows(a, width):
    b, t, _ = a.shape
    a = a.reshape(b, t // Q_BLOCK, Q_BLOCK, KV_HEADS, GQA, width)
    return a.transpose(0, 3, 1, 4, 2, 5).reshape(b, KV_HEADS, t // Q_BLOCK, GQA * Q_BLOCK, width)


def _ungroup_rows(o, b, t):
    o = o.reshape(b, KV_HEADS, t // Q_BLOCK, GQA, Q_BLOCK, N_HEAD_DIM)
    return o.transpose(0, 2, 4, 1, 3, 5).reshape(b, t, D_N)


def _sample_rows(a, width):
    b, t, _ = a.shape
    return a.reshape(b, t, KV_HEADS * GQA, width).transpose(0, 2, 1, 3).reshape(b, KV_HEADS * GQA * t, width)


def _pages_as_stored(cache):
    n, rows = cache.shape[:2]
    return cache.reshape(n, rows, KV_W).transpose(0, 2, 1)


def kernel(x_prompt, x_sample, cache_kv_cmp, cache_kv_sel, cache_kv_win, state_conv, state_C, state_n, state_m,
           page_table, g_pre, w_in, b_igate, b_fgate, w_conv, b_conv, w_cmp, pe_cmp, w_out, g_post):
    assert g_pre.shape[0] == 1, "single-layer trunk"
    bp, t_p, _ = x_prompt.shape
    bs, t_s, _ = x_sample.shape
    n_pages = page_table.shape[1]
    past = n_pages * PAGE_SIZE
    assert cache_kv_win.shape[2] == WINDOW and past >= WINDOW
    w = w_in[0]

    w_lp = jnp.concatenate([w[:, 0:2560], w[:, 3336:3848], w[:, 3872:4384]], axis=1).astype(BF16)
    w_hp = w[:, 2568:3336]
    w_ms = jnp.concatenate([w[:, 2560:2568], w[:, 3848:3872], jnp.zeros((D_MODEL, LANES - 32), F32)], axis=1)
    g_pre2 = g_pre[0].reshape(1, D_MODEL)
    cw = _compress_weights(w_cmp[0], pe_cmp[0])
    w_out_b = w_out[0].astype(BF16)
    g_post2 = g_post[0].reshape(1, D_MODEL)

    tabs_p = _rope_tables(jnp.arange(t_p, dtype=jnp.int32))
    xp2 = x_prompt.reshape(bp * t_p, D_MODEL)
    qk, v_m, o_m, z_m, q_n, q_rot, kvc, kvs, kvw, z_n, ms = _project(
        xp2, g_pre2, w_lp, w_hp, w_ms, tabs_p, t_p // PROJ_ROWS)
    r3 = lambda a: a.reshape(bp, t_p, a.shape[-1])
    qk3 = r3(qk)
    zeros = lambda *s: jnp.zeros(s, F32)
    h_m, c_p, n_p, m_p = _mlstm(qk3, r3(v_m), r3(ms), zeros(bp, SUBLANES, 2 * D_M),
                                zeros(bp, M_HEADS, M_HEAD_DIM, M_HEAD_DIM), zeros(bp, M_HEADS, M_HEAD_DIM),
                                zeros(bp, M_HEADS, LANES), w_conv[0], b_conv[0], b_igate[0], b_fgate[0],
                                MLSTM_CHUNK, MLSTM_CHUNK)
    kvc3, kvs3, kvw3 = r3(kvc), r3(kvs), r3(kvw)
    n_sub_p = t_p // D_CMP
    kv_c = _compress(kvc3.reshape(bp, n_sub_p, CMP_COLS), cw)
    n_sel_p = -(-t_p // L_SEL)
    block_of_row = (jnp.arange(t_p)[:, None] // L_SEL == jnp.arange(n_sel_p)[None, :]).astype(BF16)
    k_sel = _split_heads(kvs3, 0).astype(BF16)
    k_aug = jnp.concatenate([jnp.broadcast_to(block_of_row, (bp, KV_HEADS, t_p, n_sel_p)), k_sel,
                             jnp.zeros_like(k_sel)], axis=3)
    v_sel = _split_heads(kvs3, 1).astype(BF16)
    v_aug = jnp.concatenate([v_sel, jnp.ones((bp, KV_HEADS, t_p, 1), BF16),
                             jnp.zeros((bp, KV_HEADS, t_p, LANES - N_HEAD_DIM - 1), BF16)], axis=3)
    o_rows = _nsa_prompt(
        _group_rows(r3(q_n), N_HEAD_DIM), _group_rows(r3(q_rot), N_HEAD_DIM),
        _group_rows(r3(ms)[:, :, 8:8 + 3 * N_HEADS], 3),
        _split_heads(kv_c, 0), jnp.swapaxes(_split_heads(kv_c, 1), 2, 3),
        _importance_matrix(n_sel_p, n_sub_p).astype(BF16), k_aug, v_aug,
        _split_heads(kvw3, 0).astype(BF16), _split_heads(kvw3, 1).astype(BF16))
    o_n = _ungroup_rows(o_rows, bp, t_p).reshape(bp * t_p, D_N)
    y_p = _merge(xp2, h_m.reshape(bp * t_p, D_M), o_m, z_m, o_n, z_n, w_out_b, g_post2).reshape(bp, t_p, D_MODEL)
    kv_shape = lambda a, bb, tt: a.reshape(1, bb, tt, 2, KV_HEADS, N_HEAD_DIM)
    p_out = (kv_shape(kvc3, bp, t_p), kv_shape(kvs3, bp, t_p),
             kv_shape(kvw3[:, -min(WINDOW, t_p):], bp, min(WINDOW, t_p)),
             qk3[:, -(CONV_W - 1):][None], c_p[None], n_p[None], m_p[:, :, 0][None])

    pos_s = past + jnp.arange(t_s, dtype=jnp.int32)
    rows_s = bs * t_s
    tabs_s = tuple(jnp.tile(tb, (PROJ_ROWS // t_s, 1)) for tb in _rope_tables(pos_s))
    xs2 = x_sample.reshape(rows_s, D_MODEL)
    qk, v_m, o_m, z_m, q_n, q_rot, kvc, kvs, kvw, z_n, ms = _project(xs2, g_pre2, w_lp, w_hp, w_ms, tabs_s, 1)
    r3 = lambda a: a.reshape(bs, t_s, a.shape[-1])
    pad8 = lambda a: jnp.pad(r3(a), ((0, 0), (0, SUBLANES - t_s), (0, 0)))
    qk3 = r3(qk)
    tail = jnp.pad(state_conv[0], ((0, 0), (SUBLANES - (CONV_W - 1), 0), (0, 0)))
    m0 = jnp.broadcast_to(state_m[0][:, :, None], (bs, M_HEADS, LANES))
    h_m8, c_s, n_s, m_s = _mlstm(pad8(qk), pad8(v_m), pad8(ms), tail, state_C[0], state_n[0], m0,
                                 w_conv[0], b_conv[0], b_igate[0], b_fgate[0], SUBLANES, t_s)
    h_m = h_m8[:, :t_s].reshape(rows_s, D_M)
    conv_s = jnp.concatenate([state_conv[0], qk3], axis=1)[:, -(CONV_W - 1):]

    kvc3, kvs3 = r3(kvc), r3(kvs)
    n_sub_s = n_pages * (PAGE_SIZE // D_CMP)
    x_new = jnp.pad(kvc3, ((0, 0), (0, (-t_s) % D_CMP), (0, 0))).reshape(bs, CMP_COLS)
    fs_new = jnp.pad(_sub_proj(x_new, cw)[:, None, :], ((0, 0), (0, SUBLANES - 1), (0, 0)))
    n_sel_s = past // L_SEL + (-(-t_s // L_SEL))
    n_sel_pad = -(-n_sel_s // LANES) * LANES
    m_mat = (_importance_matrix(n_sel_pad, n_sub_s) * (jnp.arange(n_sel_pad) < n_sel_s)[:, None]).T.astype(BF16)
    n_rows = KV_HEADS * GQA * t_s
    gq = ((jnp.arange(GQA * t_s)[None, :, None] % t_s + t_s * jnp.arange(KV_HEADS)[:, None, None])
          == jnp.arange(KV_HEADS * t_s)[None, None, :]).astype(F32)
    o_c, sel = _nsa_sample_cmp(page_table, _pages_as_stored(cache_kv_cmp[0]), cw, fs_new,
                               _sample_rows(r3(q_n), N_HEAD_DIM), m_mat, gq, past, t_s, n_sel_s)
    sel_rows = jnp.broadcast_to(sel.reshape(bs, KV_HEADS, 1, t_s, n_sel_pad),
                                (bs, KV_HEADS, GQA, t_s, n_sel_pad)).reshape(bs, n_rows, n_sel_pad)
    qr_rows = _sample_rows(r3(q_rot), N_HEAD_DIM)
    head_of_row = jnp.arange(n_rows) // (GQA * t_s)
    qbd = jnp.concatenate([jnp.where((head_of_row == hh)[None, :, None], qr_rows, 0.0) for hh in range(KV_HEADS)],
                          axis=2)
    o_rows, win_t = _nsa_sample_attn(page_table, _pages_as_stored(cache_kv_sel[0]), qbd, sel_rows,
                                     _pages_as_stored(cache_kv_win[0]), pad8(kvs), pad8(kvw), o_c,
                                     _sample_rows(r3(ms)[:, :, 8:8 + 3 * N_HEADS], 3), past, t_s)
    o_n = o_rows.reshape(bs, N_HEADS, t_s, N_HEAD_DIM).transpose(0, 2, 1, 3).reshape(rows_s, D_N)
    y_s = _merge(xs2, h_m, o_m, z_m, o_n, z_n, w_out_b, g_post2).reshape(bs, t_s, D_MODEL)
    s_out = (kv_shape(kvc3, bs, t_s), kv_shape(kvs3, bs, t_s),
             kv_shape(jnp.swapaxes(win_t, 1, 2), bs, WINDOW),
             conv_s[None], c_s[None], n_s[None], m_s[:, :, 0][None])

    return (y_p, y_s) + p_out + s_out
```
